```python
import math
import jax, jax.numpy as jnp
from jax import lax
import numpy as np

D_MODEL = 1024
BATCH = 4
SEQ = 4096
DEPTH = 4
DEC_BATCH = 128
DEC_SEQ = 4
PAST_LEN = 8192
PAGE_SIZE = 128

N_META = 16
N_A_LAYERS = DEPTH // 2
N_B_LAYERS = DEPTH - N_A_LAYERS
CONV_DIM = D_MODEL
CONV_WIDTH = 3
N_HEADS = 16
QK_NOPE = 64
QK_ROPE = 32
V_HEAD = 64
Q_LORA = D_MODEL // 2
KV_LORA = D_MODEL // 4
MLA_DIM = N_HEADS * V_HEAD
ROPE_THETA = 10000.0
RMS_EPS = 1e-6
Q_BLOCK = 128
SOFTMAX_SCALE = (QK_NOPE + QK_ROPE) ** -0.5

kernel_name = "yoco_shortconv_mla_decoder_step"


def rmsnorm(x, g):
    xf = x.astype(jnp.float32)
    y = xf * lax.rsqrt(jnp.mean(xf * xf, axis=-1, keepdims=True) + RMS_EPS)
    return (y * g.astype(jnp.float32)).astype(x.dtype)


def rope(x, pos):
    half = x.shape[-1] // 2
    inv = ROPE_THETA ** (-jnp.arange(half, dtype=jnp.float32) / half)
    ang = pos.astype(jnp.float32)[:, None] * inv[None, :]
    cos = jnp.cos(ang)[None, :, None, :]
    sin = jnp.sin(ang)[None, :, None, :]
    xf = x.astype(jnp.float32)
    x1, x2 = xf[..., :half], xf[..., half:]
    return jnp.concatenate([x1 * cos - x2 * sin, x1 * sin + x2 * cos], axis=-1).astype(x.dtype)


def conv_mixer(h, prev, w_in, conv_w, w_out):
    L = h.shape[1]
    b_gate, c_gate, u, z = jnp.split(h @ w_in, 4, axis=-1)
    v = c_gate * u
    vc = jnp.concatenate([prev.astype(v.dtype), v], axis=1)
    y = conv_w[0] * vc[:, 0:L]
    for k in range(1, CONV_WIDTH):
        y = y + conv_w[k] * vc[:, k:k + L]
    out = (b_gate * y * jax.nn.silu(z)) @ w_out
    return out, vc[:, L:]


def shared_latent(s, pos, kv_norm_g, w_dkv, kv_lat_g):
    ckr = rmsnorm(s, kv_norm_g) @ w_dkv
    c = rmsnorm(ckr[..., :KV_LORA], kv_lat_g)
    kr = rope(ckr[..., None, KV_LORA:], pos)[:, :, 0]
    return c, kr


def mla_query(h, pos, w_in, q_norm_g, w_uq):
    proj = h @ w_in
    q_lat, z = proj[..., :Q_LORA], proj[..., Q_LORA:]
    q = (rmsnorm(q_lat, q_norm_g) @ w_uq).reshape(h.shape[0], h.shape[1], N_HEADS, QK_NOPE + QK_ROPE)
    return q[..., :QK_NOPE], rope(q[..., QK_NOPE:], pos), z


def attend(q, k, v, q_pos, k_pos):
    Bq, Q = q.shape[0], q.shape[1]
    qb = min(Q_BLOCK, Q)
    nb = -(-Q // qb)
    pad = nb * qb - Q
    qp = jnp.pad(q, ((0, 0), (0, pad), (0, 0), (0, 0), (0, 0)))
    pp = jnp.pad(q_pos, (0, pad), mode='edge')
    q_blocks = jnp.swapaxes(qp.reshape((Bq, nb, qb) + q.shape[2:]), 0, 1)
    p_blocks = pp.reshape(nb, qb)
    neg = jnp.finfo(jnp.float32).min

    def one_block(args):
        qi, pi = args
        s = jnp.einsum('bqghd,bkgd->bghqk', qi, k, preferred_element_type=jnp.float32) * SOFTMAX_SCALE
        s = jnp.where(k_pos[None, :] <= pi[:, None], s, neg)
        p = jax.nn.softmax(s, axis=-1)
        return jnp.einsum('bghqk,bkgv->bqghv', p.astype(v.dtype), v)

    out = lax.map(one_block, (q_blocks, p_blocks))
    out = jnp.swapaxes(out, 0, 1).reshape((Bq, nb * qb) + out.shape[3:])
    return out[:, :Q]


def setup_inputs(seed: int = 0) -> dict:
    key = jax.random.key(seed)
    ks = jax.random.split(key, 24)
    n_pages = PAST_LEN // PAGE_SIZE
    n_used = DEC_BATCH * n_pages
    n_pool = n_used + n_used // 4
    f32 = jnp.float32
    nrm = lambda k, shape, s: jax.random.normal(k, shape, f32) * s
    gain = lambda k, shape: 1.0 + 0.05 * jax.random.normal(k, shape, f32)
    page_table = jax.random.permutation(ks[0], n_pool)[:n_used].reshape(DEC_BATCH, n_pages).astype(jnp.int32)
    return {
        "x_prompt": nrm(ks[1], (BATCH, SEQ, D_MODEL), 1.0),
        "x_sample": nrm(ks[2], (DEC_BATCH, DEC_SEQ, D_MODEL), 1.0),
        "cache_ckv": nrm(ks[3], (n_pool, PAGE_SIZE, KV_LORA), 1.0),
        "cache_krope": nrm(ks[4], (n_pool, PAGE_SIZE, QK_ROPE), 1.0),
        "state_conv": nrm(ks[5], (N_A_LAYERS, DEC_BATCH, CONV_WIDTH - 1, CONV_DIM), 1.0),
        "page_table": page_table,
        "meta_tokens": nrm(ks[6], (N_META, D_MODEL), 1.0),
        "pre_norm_g": gain(ks[7], (DEPTH, D_MODEL)),
        "post_norm_g": gain(ks[8], (DEPTH, D_MODEL)),
        "w_in_conv": nrm(ks[9], (N_A_LAYERS, D_MODEL, 4 * CONV_DIM), D_MODEL ** -0.5),
        "conv_w": nrm(ks[10], (N_A_LAYERS, CONV_WIDTH, CONV_DIM), CONV_WIDTH ** -0.5),
        "w_out_conv": nrm(ks[11], (N_A_LAYERS, CONV_DIM, D_MODEL), CONV_DIM ** -0.5),
        "kv_norm_g": gain(ks[12], (D_MODEL,)),
        "w_dkv": nrm(ks[13], (D_MODEL, KV_LORA + QK_ROPE), D_MODEL ** -0.5),
        "kv_lat_norm_g": gain(ks[14], (KV_LORA,)),
        "w_uk": nrm(ks[15], (KV_LORA, N_HEADS, QK_NOPE), KV_LORA ** -0.5),
        "w_uv": nrm(ks[16], (KV_LORA, N_HEADS, V_HEAD), KV_LORA ** -0.5),
        "w_in_mla": nrm(ks[17], (N_B_LAYERS, D_MODEL, Q_LORA + MLA_DIM), D_MODEL ** -0.5),
        "q_norm_g": gain(ks[18], (N_B_LAYERS, Q_LORA)),
        "w_uq": nrm(ks[19], (N_B_LAYERS, Q_LORA, N_HEADS * (QK_NOPE + QK_ROPE)), Q_LORA ** -0.5),
        "w_out_mla": nrm(ks[20], (N_B_LAYERS, MLA_DIM, D_MODEL), MLA_DIM ** -0.5),
    }


def reference(x_prompt, x_sample, cache_ckv, cache_krope, state_conv, page_table, meta_tokens,
              pre_norm_g, post_norm_g, w_in_conv, conv_w, w_out_conv, kv_norm_g, w_dkv,
              kv_lat_norm_g, w_uk, w_uv, w_in_mla, q_norm_g, w_uq, w_out_mla):

    def trunk(x, pos, conv_prev, past_c, past_kr, k_pos):
        absorbed = past_c is not None
        new_conv = []
        keys = vals = c_new = kr_new = None
        for l in range(DEPTH):
            h = rmsnorm(x, pre_norm_g[l])
            if l < N_A_LAYERS:
                m, st = conv_mixer(h, conv_prev[l], w_in_conv[l], conv_w[l], w_out_conv[l])
                new_conv.append(st)
            else:
                j = l - N_A_LAYERS
                q_nope, q_rope, z = mla_query(h, pos, w_in_mla[j], q_norm_g[j], w_uq[j])
                Bx, L = h.shape[0], h.shape[1]
                if absorbed:
                    q_lat = jnp.einsum('blhd,chd->blhc', q_nope, w_uk)
                    q = jnp.concatenate([q_lat, q_rope], axis=-1)[:, :, None]
                    o_lat = attend(q, keys, vals, pos, k_pos)[:, :, 0]
                    o = jnp.einsum('blhc,chv->blhv', o_lat, w_uv)
                else:
                    q = jnp.concatenate([q_nope, q_rope], axis=-1)[:, :, :, None]
                    o = attend(q, keys, vals, pos, k_pos)[:, :, :, 0]
                m = (o.reshape(Bx, L, MLA_DIM) * jax.nn.silu(z)) @ w_out_mla[j]
            x = x + rmsnorm(m, post_norm_g[l])
            if l == N_A_LAYERS - 1:
                c_new, kr_new = shared_latent(x, pos, kv_norm_g, w_dkv, kv_lat_norm_g)
                if absorbed:
                    c_all = jnp.concatenate([past_c.astype(c_new.dtype), c_new], axis=1)
                    kr_all = jnp.concatenate([past_kr.astype(kr_new.dtype), kr_new], axis=1)
                    keys = jnp.concatenate([c_all, kr_all], axis=-1)[:, :, None]
                    vals = c_all[:, :, None]
                else:
                    k_nope = jnp.einsum('btc,chd->bthd', c_new, w_uk)
                    k_rope_h = jnp.broadcast_to(kr_new[:, :, None], kr_new.shape[:2] + (N_HEADS, QK_ROPE))
                    keys = jnp.concatenate([k_nope, k_rope_h], axis=-1)
                    vals = jnp.einsum('btc,chv->bthv', c_new, w_uv)
        return x, c_new, kr_new, jnp.stack(new_conv)

    bp = x_prompt.shape[0]
    t_total = x_prompt.shape[1] + N_META
    meta = jnp.broadcast_to(meta_tokens[None].astype(x_prompt.dtype), (bp, N_META, D_MODEL))
    xp = jnp.concatenate([meta, x_prompt], axis=1)
    pos_p = jnp.arange(t_total, dtype=jnp.int32)
    conv0 = jnp.zeros((N_A_LAYERS, bp, CONV_WIDTH - 1, CONV_DIM), x_prompt.dtype)
    hp, ckv_prompt, krope_prompt, conv_prompt = trunk(xp, pos_p, conv0, None, None, pos_p)
    y_prompt = hp[:, N_META:]

    bs, ls = x_sample.shape[0], x_sample.shape[1]
    past_len = page_table.shape[1] * cache_ckv.shape[1]
    past_c = cache_ckv[page_table].reshape(bs, past_len, KV_LORA)
    past_kr = cache_krope[page_table].reshape(bs, past_len, QK_ROPE)
    pos_s = past_len + jnp.arange(ls, dtype=jnp.int32)
    k_pos_s = jnp.arange(past_len + ls, dtype=jnp.int32)
    y_sample, ckv_sample, krope_sample, conv_sample = trunk(x_sample, pos_s, state_conv, past_c, past_kr, k_pos_s)

    return (y_prompt, y_sample, ckv_prompt, krope_prompt, conv_prompt, ckv_sample, krope_sample, conv_sample)
```

```python
import functools

import jax
import jax.numpy as jnp
from jax import lax
from jax.experimental import pallas as pl
from jax.experimental.pallas import tpu as pltpu

N_META = 16
N_HEADS = 16
QK_NOPE = 64
QK_ROPE = 32
V_HEAD = 64
ROPE_THETA = 10000.0
RMS_EPS = 1e-6
CONV_WIDTH = 3
SOFTMAX_SCALE = (QK_NOPE + QK_ROPE) ** -0.5

LANES = 128
SUBLANES = 8
HALF_ROPE = QK_ROPE // 2
VMEM_LIMIT = 56 * 1024 * 1024

ROW_TILE = 512
ATTN_TILE = 512
PAGES_PER_STEP = 16

bf16 = jnp.bfloat16
f32 = jnp.float32


def _rms(x, g):
    return x * lax.rsqrt(jnp.mean(x * x, axis=-1, keepdims=True) + RMS_EPS) * g


def _dot(a, b):
    return jnp.dot(a, b, preferred_element_type=f32)


def _dot_nt(a, b):
    return lax.dot_general(a, b, (((1,), (1,)), ((), ())), preferred_element_type=f32)


def _rope_block(blk, cos, s1, s2):
    return (blk * cos + pltpu.roll(blk, LANES - HALF_ROPE, 1) * s1 + pltpu.roll(blk, HALF_ROPE, 1) * s2)


def _params(*sem):
    return pltpu.CompilerParams(dimension_semantics=sem, vmem_limit_bytes=VMEM_LIMIT)


def _const_spec(shape):
    nd = len(shape)
    return pl.BlockSpec(shape, lambda *_: (0,) * nd)


def _conv_layer_kernel(x_ref, init_ref, pre_g_ref, post_g_ref, w_in_ref, cw_ref, w_out_ref,
                       xo_ref, st_ref, vbuf, *, tile, off, shift, c_dim):
    i = pl.program_id(1)

    @pl.when(i == 0)
    def _():
        vbuf[0:off, :] = init_ref[0]

    x = x_ref[...]
    h = _rms(x, pre_g_ref[...]).astype(bf16)

    def proj(k):
        return _dot(h, w_in_ref[:, k * c_dim:(k + 1) * c_dim])

    vbuf[off:off + tile, :] = proj(1) * proj(2)
    cw = cw_ref[...]
    y = cw[0:1] * vbuf[off - 2 * shift:off - 2 * shift + tile, :]
    y = y + cw[1:2] * vbuf[off - shift:off - shift + tile, :]
    y = y + cw[2:3] * vbuf[off:off + tile, :]
    z = proj(3)
    g = (proj(0) * y * jax.nn.silu(z)).astype(bf16)
    m = _dot(g, w_out_ref[...])
    xo_ref[...] = x + _rms(m, post_g_ref[...])

    @pl.when(i == pl.num_programs(1) - 1)
    def _():
        st_ref[0] = vbuf[off + tile - 2 * shift:off + tile, :]

    vbuf[0:off, :] = vbuf[tile:tile + off, :]


def _conv_layer(x, init, pre_g, post_g, w_in, cw, w_out, *, nseq, tile, shift):
    rows, d = x.shape
    c_dim = cw.shape[1]
    t = rows // nseq
    nt = t // tile
    off = init.shape[1]
    ninit = init.shape[0]
    kern = functools.partial(_conv_layer_kernel, tile=tile, off=off, shift=shift, c_dim=c_dim)
    return pl.pallas_call(
        kern,
        grid=(nseq, nt),
        in_specs=[
            pl.BlockSpec((tile, d), lambda b, i: (b * nt + i, 0)),
            pl.BlockSpec((1, off, c_dim), (lambda b, i: (b, 0, 0)) if ninit > 1 else (lambda b, i: (0, 0, 0))),
            _const_spec((1, d)), _const_spec((1, d)),
            _const_spec(w_in.shape), _const_spec(cw.shape), _const_spec(w_out.shape),
        ],
        out_specs=[
            pl.BlockSpec((tile, d), lambda b, i: (b * nt + i, 0)),
            pl.BlockSpec((1, 2 * shift, c_dim), lambda b, i: (b, 0, 0)),
        ],
        out_shape=[jax.ShapeDtypeStruct((rows, d), f32),
                   jax.ShapeDtypeStruct((nseq, 2 * shift, c_dim), f32)],
        scratch_shapes=[pltpu.VMEM((off + tile, c_dim), f32)],
        compiler_params=_params("arbitrary", "arbitrary"),
        name="conv_layer",
    )(x, init, pre_g, post_g, w_in, cw, w_out)


def _latent_kernel(*refs, kv_lora, with_kv):
    if with_kv:
        (x_ref, g_ref, w_dkv_ref, lat_g_ref, cos_ref, s1_ref, s2_ref, w_uk_ref, w_uv_ref,
         c_ref, kr_ref, k_ref, v_ref) = refs
    else:
        x_ref, g_ref, w_dkv_ref, lat_g_ref, cos_ref, s1_ref, s2_ref, c_ref, kr_ref = refs
    xn = _rms(x_ref[...], g_ref[...]).astype(bf16)
    ckr = _dot(xn, w_dkv_ref[...])
    c = _rms(ckr[:, :kv_lora], lat_g_ref[...])
    krb = _rope_block(ckr[:, kv_lora:kv_lora + LANES], cos_ref[...], s1_ref[...], s2_ref[...])
    c_ref[...] = c
    kr_ref[...] = krb[:, :QK_ROPE]
    if with_kv:
        cb = c.astype(bf16)
        kn = _dot(cb, w_uk_ref[...])
        for h in range(N_HEADS):
            k_ref[:, h * LANES:(h + 1) * LANES] = (kn[:, h * LANES:(h + 1) * LANES] + krb).astype(bf16)
        v_ref[...] = _dot(cb, w_uv_ref[...]).astype(bf16)


def _latent(x, g, w_dkv_pad, lat_g, tabs, w_uk_pad=None, w_uv=None, *, nseq, tile):
    rows, d = x.shape
    kv_lora = lat_g.shape[1]
    nt = rows // nseq // tile
    with_kv = w_uk_pad is not None
    row_spec = lambda w: pl.BlockSpec((tile, w), lambda b, i: (b * nt + i, 0))
    tab_spec = pl.BlockSpec((tile, LANES), lambda b, i: (i, 0))
    in_specs = [row_spec(d), _const_spec((1, d)), _const_spec(w_dkv_pad.shape), _const_spec((1, kv_lora)),
                tab_spec, tab_spec, tab_spec]
    args = [x, g, w_dkv_pad, lat_g, *tabs]
    out_specs = [row_spec(kv_lora), row_spec(QK_ROPE)]
    out_shape = [jax.ShapeDtypeStruct((rows, kv_lora), f32), jax.ShapeDtypeStruct((rows, QK_ROPE), f32)]
    if with_kv:
        in_specs += [_const_spec(w_uk_pad.shape), _const_spec(w_uv.shape)]
        args += [w_uk_pad, w_uv]
        out_specs += [row_spec(N_HEADS * LANES), row_spec(N_HEADS * V_HEAD)]
        out_shape += [jax.ShapeDtypeStruct((rows, N_HEADS * LANES), bf16),
                      jax.ShapeDtypeStruct((rows, N_HEADS * V_HEAD), bf16)]
    return pl.pallas_call(
        functools.partial(_latent_kernel, kv_lora=kv_lora, with_kv=with_kv),
        grid=(nseq, nt), in_specs=in_specs, out_specs=out_specs, out_shape=out_shape,
        compiler_params=_params("arbitrary", "arbitrary"),
        name="latent_kv" if with_kv else "latent",
    )(*args)


def _mla_query_kernel(*refs, q_lora, absorbed, rows):
    if absorbed:
        (x_ref, pre_g_ref, w_in_ref, qg_ref, w_uq_ref, cos_ref, s1_ref, s2_ref, w_ukt_ref,
         ql_ref, qr_ref, z_ref) = refs
    else:
        x_ref, pre_g_ref, w_in_ref, qg_ref, w_uq_ref, cos_ref, s1_ref, s2_ref, q_ref, z_ref = refs
    h = _rms(x_ref[...], pre_g_ref[...]).astype(bf16)
    q_lat = _dot(h, w_in_ref[:, :q_lora])
    z_ref[...] = _dot(h, w_in_ref[:, q_lora:])
    qn = _rms(q_lat, qg_ref[...]).astype(bf16)
    q = _dot(qn, w_uq_ref[...])
    cos, s1, s2 = cos_ref[...], s1_ref[...], s2_ref[...]
    for hd in range(N_HEADS):
        blk = _rope_block(q[:, hd * LANES:(hd + 1) * LANES], cos, s1, s2)
        if absorbed:
            ql = _dot(blk.astype(bf16), w_ukt_ref[hd])
            for c in range(ql.shape[1] // LANES):
                ql_ref[c, pl.ds(hd, rows, stride=N_HEADS), :] = ql[:, c * LANES:(c + 1) * LANES]
            qr_ref[pl.ds(hd, rows, stride=N_HEADS), :] = blk
        else:
            q_ref[:, hd * LANES:(hd + 1) * LANES] = blk.astype(bf16)


def _mla_query(x, pre_g, w_in, qg, w_uq_pad, tabs, w_ukt_pad=None, *, nseq, tile):
    rows, d = x.shape
    q_lora = qg.shape[1]
    z_dim = w_in.shape[1] - q_lora
    nt = rows // nseq // tile
    absorbed = w_ukt_pad is not None
    row_spec = lambda w: pl.BlockSpec((tile, w), lambda b, i: (b * nt + i, 0))
    tab_spec = pl.BlockSpec((tile, LANES), lambda b, i: (i, 0))
    in_specs = [row_spec(d), _const_spec((1, d)), _const_spec(w_in.shape), _const_spec((1, q_lora)),
                _const_spec(w_uq_pad.shape), tab_spec, tab_spec, tab_spec]
    args = [x, pre_g, w_in, qg, w_uq_pad, *tabs]
    if absorbed:
        assert nseq == 1 and nt == 1
        kv_lora = w_ukt_pad.shape[2]
        in_specs.append(_const_spec(w_ukt_pad.shape))
        args.append(w_ukt_pad)
        ql_shape = (kv_lora // LANES, rows * N_HEADS, LANES)
        out_specs = [_const_spec(ql_shape), _const_spec((rows * N_HEADS, LANES)), row_spec(z_dim)]
        out_shape = [jax.ShapeDtypeStruct(ql_shape, f32),
                     jax.ShapeDtypeStruct((rows * N_HEADS, LANES), f32),
                     jax.ShapeDtypeStruct((rows, z_dim), f32)]
    else:
        out_specs = [row_spec(N_HEADS * LANES), row_spec(z_dim)]
        out_shape = [jax.ShapeDtypeStruct((rows, N_HEADS * LANES), bf16),
                     jax.ShapeDtypeStruct((rows, z_dim), f32)]
    return pl.pallas_call(
        functools.partial(_mla_query_kernel, q_lora=q_lora, absorbed=absorbed, rows=rows),
        grid=(nseq, nt), in_specs=in_specs, out_specs=out_specs, out_shape=out_shape,
        compiler_params=_params("arbitrary", "arbitrary"),
        name="mla_query_absorbed" if absorbed else "mla_query",
    )(*args)


def _prompt_attn_kernel(q_ref, k_ref, v_ref, mk_ref, mv_ref, o_ref, *, tile):
    i = pl.program_id(2)
    row = lax.broadcasted_iota(jnp.int32, (tile, tile), 0)
    col = lax.broadcasted_iota(jnp.int32, (tile, tile), 1)
    causal = col <= row
    neg = jnp.finfo(f32).min
    outs = []
    for hh in range(2):
        q = q_ref[0, :, hh * LANES:(hh + 1) * LANES]

        def scores(start, size):
            return _dot_nt(q, k_ref[0, pl.ds(start, size), hh * LANES:(hh + 1) * LANES])

        d0 = pl.multiple_of(i * tile, tile)
        s = jnp.where(causal, scores(d0, tile), neg)
        m = jnp.max(s, axis=-1, keepdims=True)
        p = jnp.exp(s - m)
        l = jnp.sum(p, axis=-1, keepdims=True)
        acc = _dot(p.astype(bf16), v_ref[0, pl.ds(d0, tile), :])

        def update(carry, s, v):
            m, l, acc = carry
            m_new = jnp.maximum(m, jnp.max(s, axis=-1, keepdims=True))
            alpha = jnp.exp(m - m_new)
            p = jnp.exp(s - m_new)
            l = alpha * l + jnp.sum(p, axis=-1, keepdims=True)
            acc = alpha * acc + _dot(p.astype(bf16), v)
            return m_new, l, acc

        def body(j, carry):
            start = pl.multiple_of(j * tile, tile)
            return update(carry, scores(start, tile), v_ref[0, pl.ds(start, tile), :])

        carry = lax.fori_loop(0, i, body, (m, l, acc))
        s_meta = _dot_nt(q, mk_ref[:, hh * LANES:(hh + 1) * LANES])
        m, l, acc = update(carry, s_meta, mv_ref[...])
        outs.append(acc / l)
    lane = lax.broadcasted_iota(jnp.int32, (tile, LANES), 1)
    o_ref[0] = jnp.where(lane < V_HEAD, outs[0], outs[1])


def _prompt_attn(q, k, v, mk, mv, *, tile):
    b, t, _ = q.shape
    nq = t // tile
    return pl.pallas_call(
        functools.partial(_prompt_attn_kernel, tile=tile),
        grid=(b, N_HEADS // 2, nq),
        in_specs=[
            pl.BlockSpec((1, tile, 2 * LANES), lambda b, g, i: (b, i, g)),
            pl.BlockSpec((1, t, 2 * LANES), lambda b, g, i: (b, 0, g)),
            pl.BlockSpec((1, t, LANES), lambda b, g, i: (b, 0, g)),
            pl.BlockSpec((N_META, 2 * LANES), lambda b, g, i: (0, g)),
            pl.BlockSpec((N_META, LANES), lambda b, g, i: (0, g)),
        ],
        out_specs=pl.BlockSpec((1, tile, LANES), lambda b, g, i: (b, i, g)),
        out_shape=jax.ShapeDtypeStruct((b, t, N_HEADS * V_HEAD), f32),
        compiler_params=_params("arbitrary", "arbitrary", "arbitrary"),
        name="prompt_attn",
    )(q, k, v, mk, mv)


def _sample_attn_kernel(pt_ref, ql_ref, qr_ref, cn_ref, krn_ref, *rest, pages, page, l_new):
    del pt_ref
    ckv_refs = rest[:pages]
    kr_refs = rest[pages:2 * pages]
    o_ref = rest[2 * pages]
    m_sc, l_sc, acc_sc, kbuf, krbuf = rest[2 * pages + 1:]
    j = pl.program_id(1)
    n_half = ql_ref.shape[0]
    ql = jnp.concatenate([ql_ref[c] for c in range(n_half)], axis=-1).astype(bf16)
    qr = qr_ref[:, :QK_ROPE].astype(bf16)
    nq = ql.shape[0]

    @pl.when(j == 0)
    def _():
        cn = cn_ref[0].astype(bf16)
        krn = krn_ref[0].astype(bf16)
        npad = cn.shape[0]
        s = _dot_nt(ql, cn) + _dot_nt(qr, krn)
        row = lax.broadcasted_iota(jnp.int32, (nq, npad), 0)
        col = lax.broadcasted_iota(jnp.int32, (nq, npad), 1)
        s = jnp.where(col * N_HEADS <= row, s, jnp.finfo(f32).min)
        m = jnp.max(s, axis=-1, keepdims=True)
        p = jnp.exp(s - m)
        m_sc[...] = m
        l_sc[...] = jnp.sum(p, axis=-1, keepdims=True)
        acc_sc[...] = _dot(p.astype(bf16), cn)

    for p_i in range(pages):
        kbuf[p_i * page:(p_i + 1) * page, :] = ckv_refs[p_i][0].astype(bf16)
        krbuf[p_i * page:(p_i + 1) * page, :] = kr_refs[p_i][0].astype(bf16)
    kb = kbuf[...]
    s = _dot_nt(ql, kb) + _dot_nt(qr, krbuf[...])
    m_prev = m_sc[...]
    m_new = jnp.maximum(m_prev, jnp.max(s, axis=-1, keepdims=True))
    alpha = jnp.exp(m_prev - m_new)
    p = jnp.exp(s - m_new)
    l_sc[...] = alpha * l_sc[...] + jnp.sum(p, axis=-1, keepdims=True)
    acc_sc[...] = alpha * acc_sc[...] + _dot(p.astype(bf16), kb)
    m_sc[...] = m_new

    @pl.when(j == pl.num_programs(1) - 1)
    def _():
        o = acc_sc[...] / l_sc[...]
        for c in range(n_half):
            o_ref[c] = o[:, c * LANES:(c + 1) * LANES]


def _sample_attn(page_table, q_lat, q_rope, c_new, kr_new, cache_ckv, cache_krope, *, pages):
    nreq, n_pages = page_table.shape
    _, page, kv_lora = cache_ckv.shape
    n_half = q_lat.shape[0]
    nq = q_lat.shape[1] // nreq
    lpad = c_new.shape[1]
    nchunk = n_pages // pages
    pt_flat = page_table.reshape(-1)

    def page_map(p_i):
        return lambda b, j, pt: (pt[b * n_pages + j * pages + p_i], 0, 0)

    in_specs = [
        pl.BlockSpec((n_half, nq, LANES), lambda b, j, pt: (0, b, 0)),
        pl.BlockSpec((nq, LANES), lambda b, j, pt: (b, 0)),
        pl.BlockSpec((1, lpad, kv_lora), lambda b, j, pt: (b, 0, 0)),
        pl.BlockSpec((1, lpad, QK_ROPE), lambda b, j, pt: (b, 0, 0)),
    ]
    in_specs += [pl.BlockSpec((1, page, kv_lora), page_map(p_i)) for p_i in range(pages)]
    in_specs += [pl.BlockSpec((1, page, QK_ROPE), page_map(p_i)) for p_i in range(pages)]
    grid_spec = pltpu.PrefetchScalarGridSpec(
        num_scalar_prefetch=1, grid=(nreq, nchunk), in_specs=in_specs,
        out_specs=pl.BlockSpec((n_half, nq, LANES), lambda b, j, pt: (0, b, 0)),
        scratch_shapes=[pltpu.VMEM((nq, 1), f32), pltpu.VMEM((nq, 1), f32), pltpu.VMEM((nq, kv_lora), f32),
                        pltpu.VMEM((pages * page, kv_lora), bf16), pltpu.VMEM((pages * page, QK_ROPE), bf16)],
    )
    return pl.pallas_call(
        functools.partial(_sample_attn_kernel, pages=pages, page=page, l_new=nq // N_HEADS),
        grid_spec=grid_spec,
        out_shape=jax.ShapeDtypeStruct((n_half, nreq * nq, LANES), f32),
        compiler_params=_params("arbitrary", "arbitrary"),
        name="sample_attn",
    )(pt_flat, q_lat, q_rope, c_new, kr_new, *([cache_ckv] * pages), *([cache_krope] * pages))


def _mla_out_kernel(*refs, absorbed, rows):
    if absorbed:
        ol_ref, w_uv_ref, z_ref, x_ref, w_out_ref, post_g_ref, xo_ref = refs
        def head_rows(hd):
            halves = [ol_ref[c, pl.ds(hd, rows, stride=N_HEADS), :] for c in range(ol_ref.shape[0])]
            return jnp.concatenate(halves, axis=-1).astype(bf16)

        parts = []
        for g in range(N_HEADS // 2):
            parts.append(_dot(head_rows(2 * g), w_uv_ref[2 * g]) + _dot(head_rows(2 * g + 1), w_uv_ref[2 * g + 1]))
        o = jnp.concatenate(parts, axis=-1)
    else:
        o_ref, z_ref, x_ref, w_out_ref, post_g_ref, xo_ref = refs
        o = o_ref[...]
    g = (o * jax.nn.silu(z_ref[...])).astype(bf16)
    m = _dot(g, w_out_ref[...])
    xo_ref[...] = x_ref[...] + _rms(m, post_g_ref[...])


def _mla_out(o, z, x, w_out, post_g, w_uv_pad=None, *, tile):
    rows, d = x.shape
    z_dim = z.shape[1]
    nt = rows // tile
    absorbed = w_uv_pad is not None
    row_spec = lambda w: pl.BlockSpec((tile, w), lambda i: (i, 0))
    if absorbed:
        assert nt == 1
        in_specs = [_const_spec(o.shape), _const_spec(w_uv_pad.shape)]
        args = [o, w_uv_pad]
    else:
        in_specs = [row_spec(z_dim)]
        args = [o]
    in_specs += [row_spec(z_dim), row_spec(d), _const_spec(w_out.shape), _const_spec((1, d))]
    args += [z, x, w_out, post_g]
    return pl.pallas_call(
        functools.partial(_mla_out_kernel, absorbed=absorbed, rows=rows),
        grid=(nt,), in_specs=in_specs, out_specs=row_spec(d),
        out_shape=jax.ShapeDtypeStruct((rows, d), f32),
        compiler_params=_params("arbitrary"),
        name="mla_out_absorbed" if absorbed else "mla_out",
    )(*args)


def _rope_tables(pos, scale):
    inv = ROPE_THETA ** (-jnp.arange(HALF_ROPE, dtype=f32) / HALF_ROPE)
    ang = pos.astype(f32)[:, None] * inv[None, :]
    cos, sin = jnp.cos(ang), jnp.sin(ang)
    r = pos.shape[0]
    zeros = lambda w: jnp.zeros((r, w), f32)
    c = jnp.concatenate([cos, cos, zeros(LANES - QK_ROPE - QK_NOPE), jnp.ones((r, QK_NOPE), f32)], axis=-1)
    s1 = jnp.concatenate([-sin, zeros(LANES - HALF_ROPE)], axis=-1)
    s2 = jnp.concatenate([zeros(HALF_ROPE), sin, zeros(LANES - QK_ROPE)], axis=-1)
    return c * scale, s1 * scale, s2 * scale


def _head_blocks(w_rope, w_nope):
    pad = jnp.zeros(w_nope.shape[:-1] + (LANES - QK_ROPE - QK_NOPE,), w_nope.dtype)
    blk = jnp.concatenate([w_rope, pad, w_nope], axis=-1)
    return blk.reshape(blk.shape[:-2] + (N_HEADS * LANES,))


def kernel(x_prompt, x_sample, cache_ckv, cache_krope, state_conv, page_table, meta_tokens,
           pre_norm_g, post_norm_g, w_in_conv, conv_w, w_out_conv, kv_norm_g, w_dkv,
           kv_lat_norm_g, w_uk, w_uv, w_in_mla, q_norm_g, w_uq, w_out_mla):
    bp, seq, d = x_prompt.shape
    bs, ls, _ = x_sample.shape
    n_a = w_in_conv.shape[0]
    n_b = w_in_mla.shape[0]
    c_dim = conv_w.shape[2]
    kv_lora = kv_lat_norm_g.shape[0]
    q_lora = q_norm_g.shape[1]
    past_len = page_table.shape[1] * cache_ckv.shape[1]

    row = lambda v: v.reshape(1, -1).astype(f32)
    w_in_conv_b = w_in_conv.astype(bf16)
    w_out_conv_b = w_out_conv.astype(bf16)
    w_in_mla_b = w_in_mla.astype(bf16)
    w_out_mla_b = w_out_mla.astype(bf16)
    w_dkv_pad = jnp.concatenate([w_dkv, jnp.zeros((d, LANES - QK_ROPE), w_dkv.dtype)], axis=-1).astype(bf16)
    uq = w_uq.reshape(n_b, q_lora, N_HEADS, QK_NOPE + QK_ROPE)
    w_uq_pad = _head_blocks(uq[..., QK_NOPE:], uq[..., :QK_NOPE]).astype(bf16)
    w_uk_blk = _head_blocks(jnp.zeros((kv_lora, N_HEADS, QK_ROPE), w_uk.dtype), w_uk)
    w_uk_pad = w_uk_blk.astype(bf16)
    w_ukt_pad = jnp.transpose(w_uk_blk.reshape(kv_lora, N_HEADS, LANES), (1, 2, 0)).astype(bf16)
    w_uv_flat = w_uv.reshape(kv_lora, N_HEADS * V_HEAD).astype(bf16)
    uv = jnp.transpose(w_uv, (1, 0, 2))
    zv = jnp.zeros_like(uv)
    even = (jnp.arange(N_HEADS) % 2 == 0)[:, None, None]
    w_uv_pad = jnp.concatenate([jnp.where(even, uv, zv), jnp.where(even, zv, uv)], axis=-1).astype(bf16)

    def trunk_a(x, inits, *, nseq, tile, shift):
        states = []
        for l in range(n_a):
            x, st = _conv_layer(x, inits[l], row(pre_norm_g[l]), row(post_norm_g[l]), w_in_conv_b[l],
                                conv_w[l], w_out_conv_b[l], nseq=nseq, tile=tile, shift=shift)
            states.append(st)
        return x, states

    zero_init = jnp.zeros((1, SUBLANES, c_dim), f32)
    xm, meta_states = trunk_a(meta_tokens.astype(f32), [zero_init] * n_a, nseq=1, tile=N_META, shift=1)
    tabs_meta = _rope_tables(jnp.arange(N_META, dtype=jnp.int32), 1.0)
    c_meta, kr_meta, k_meta, v_meta = _latent(xm, row(kv_norm_g), w_dkv_pad, row(kv_lat_norm_g), tabs_meta,
                                              w_uk_pad, w_uv_flat, nseq=1, tile=N_META)

    tile_p = min(ROW_TILE, seq)
    inits_p = [jnp.concatenate([jnp.zeros((1, SUBLANES - 2, c_dim), f32), st], axis=1) for st in meta_states]
    xp, prompt_states = trunk_a(x_prompt.reshape(bp * seq, d), inits_p, nseq=bp, tile=tile_p, shift=1)
    pos_p = N_META + jnp.arange(seq, dtype=jnp.int32)
    c_p, kr_p, k_p, v_p = _latent(xp, row(kv_norm_g), w_dkv_pad, row(kv_lat_norm_g), _rope_tables(pos_p, 1.0),
                                  w_uk_pad, w_uv_flat, nseq=bp, tile=tile_p)
    tabs_q = _rope_tables(pos_p, SOFTMAX_SCALE)
    attn_tile = min(ATTN_TILE, seq)
    for j in range(n_b):
        l = n_a + j
        q, z = _mla_query(xp, row(pre_norm_g[l]), w_in_mla_b[j], row(q_norm_g[j]), w_uq_pad[j], tabs_q,
                          nseq=bp, tile=tile_p)
        o = _prompt_attn(q.reshape(bp, seq, -1), k_p.reshape(bp, seq, -1), v_p.reshape(bp, seq, -1),
                         k_meta, v_meta, tile=attn_tile)
        xp = _mla_out(o.reshape(bp * seq, -1), z, xp, w_out_mla_b[j], row(post_norm_g[l]), tile=tile_p)
    y_prompt = xp.reshape(bp, seq, d)
    bcast = lambda a: jnp.broadcast_to(a[None], (bp,) + a.shape)
    ckv_prompt = jnp.concatenate([bcast(c_meta), c_p.reshape(bp, seq, kv_lora)], axis=1)
    krope_prompt = jnp.concatenate([bcast(kr_meta), kr_p.reshape(bp, seq, QK_ROPE)], axis=1)
    conv_prompt = jnp.stack([st for st in prompt_states])

    rs = bs * ls
    xs = jnp.transpose(x_sample, (1, 0, 2)).reshape(rs, d)
    inits_s = [jnp.transpose(state_conv[l], (1, 0, 2)).reshape(1, (CONV_WIDTH - 1) * bs, c_dim) for l in range(n_a)]
    xs, sample_states = trunk_a(xs, inits_s, nseq=1, tile=rs, shift=bs)
    conv_sample = jnp.stack([jnp.transpose(st.reshape(CONV_WIDTH - 1, bs, c_dim), (1, 0, 2)) for st in sample_states])
    xs = jnp.transpose(xs.reshape(ls, bs, d), (1, 0, 2)).reshape(rs, d)
    pos_s = jnp.tile(past_len + jnp.arange(ls, dtype=jnp.int32), bs)
    c_s, kr_s = _latent(xs, row(kv_norm_g), w_dkv_pad, row(kv_lat_norm_g), _rope_tables(pos_s, 1.0),
                        nseq=1, tile=rs)
    lpad = 16
    c_new = jnp.pad(c_s.reshape(bs, ls, kv_lora), ((0, 0), (0, lpad - ls), (0, 0)))
    kr_new = jnp.pad(kr_s.reshape(bs, ls, QK_ROPE), ((0, 0), (0, lpad - ls), (0, 0)))
    tabs_qs = _rope_tables(pos_s, SOFTMAX_SCALE)
    pages = min(PAGES_PER_STEP, page_table.shape[1])
    for j in range(n_b):
        l = n_a + j
        q_lat, q_rope, z = _mla_query(xs, row(pre_norm_g[l]), w_in_mla_b[j], row(q_norm_g[j]), w_uq_pad[j],
                                      tabs_qs, w_ukt_pad, nseq=1, tile=rs)
        o_lat = _sample_attn(page_table, q_lat, q_rope, c_new, kr_new, cache_ckv, cache_krope, pages=pages)
        xs = _mla_out(o_lat, z, xs, w_out_mla_b[j], row(post_norm_g[l]), w_uv_pad, tile=rs)
    y_sample = xs.reshape(bs, ls, d)
    ckv_sample = c_s.reshape(bs, ls, kv_lora)
    krope_sample = kr_s.reshape(bs, ls, QK_ROPE)

    return (y_prompt, y_sample, ckv_prompt, krope_prompt, conv_prompt, ckv_sample, krope_sample, conv_sample)
```

```python
import functools

import jax
import jax.numpy as jnp
from jax import lax
from jax.experimental import pallas as pl
from jax.experimental.pallas import tpu as pltpu

N_META = 16
N_HEADS = 16
QK_NOPE = 64
QK_ROPE = 32
V_HEAD = 64
ROPE_THETA = 10000.0
RMS_EPS = 1e-6
CONV_WIDTH = 3
SOFTMAX_SCALE = (QK_NOPE + QK_ROPE) ** -0.5
LOG2_E = 1.4426950408889634

LANES = 128
SUBLANES = 8
HALF_ROPE = QK_ROPE // 2
VMEM_LIMIT = 56 * 1024 * 1024

ROW_TILE = 512
ATTN_TILE = 512
ATTN_HEADS_PER_STEP = 4

bf16 = jnp.bfloat16
f32 = jnp.float32


def _rms(x, g):
    return x * lax.rsqrt(jnp.mean(x * x, axis=-1, keepdims=True) + RMS_EPS) * g


def _dot(a, b):
    return jnp.dot(a, b, preferred_element_type=f32)


def _dot_nt(a, b):
    return lax.dot_general(a, b, (((1,), (1,)), ((), ())), preferred_element_type=f32)


def _rope_block(blk, cos, s1, s2):
    return (blk * cos + pltpu.roll(blk, LANES - HALF_ROPE, 1) * s1 + pltpu.roll(blk, HALF_ROPE, 1) * s2)


def _params(*sem):
    return pltpu.CompilerParams(dimension_semantics=sem, vmem_limit_bytes=VMEM_LIMIT)


def _const_spec(shape):
    nd = len(shape)
    return pl.BlockSpec(shape, lambda *_: (0,) * nd)


def _conv_layer_kernel(x_ref, init_ref, pre_g_ref, post_g_ref, w_in_ref, cw_ref, w_out_ref,
                       xo_ref, st_ref, vbuf, *, tile, off, shift, c_dim):
    i = pl.program_id(1)

    @pl.when(i == 0)
    def _():
        vbuf[0:off, :] = init_ref[0]

    x = x_ref[...]
    h = _rms(x, pre_g_ref[...]).astype(bf16)

    def proj(k):
        return _dot(h, w_in_ref[:, k * c_dim:(k + 1) * c_dim])

    vbuf[off:off + tile, :] = proj(1) * proj(2)
    cw = cw_ref[...]
    y = cw[0:1] * vbuf[off - 2 * shift:off - 2 * shift + tile, :]
    y = y + cw[1:2] * vbuf[off - shift:off - shift + tile, :]
    y = y + cw[2:3] * vbuf[off:off + tile, :]
    z = proj(3)
    g = (proj(0) * y * jax.nn.silu(z)).astype(bf16)
    m = _dot(g, w_out_ref[...])
    xo_ref[...] = x + _rms(m, post_g_ref[...])

    @pl.when(i == pl.num_programs(1) - 1)
    def _():
        st_ref[0] = vbuf[off + tile - 2 * shift:off + tile, :]

    vbuf[0:off, :] = vbuf[tile:tile + off, :]


def _conv_layer(x, init, pre_g, post_g, w_in, cw, w_out, *, nseq, tile, shift):
    rows, d = x.shape
    c_dim = cw.shape[1]
    t = rows // nseq
    nt = t // tile
    off = init.shape[1]
    ninit = init.shape[0]
    kern = functools.partial(_conv_layer_kernel, tile=tile, off=off, shift=shift, c_dim=c_dim)
    return pl.pallas_call(
        kern,
        grid=(nseq, nt),
        in_specs=[
            pl.BlockSpec((tile, d), lambda b, i: (b * nt + i, 0)),
            pl.BlockSpec((1, off, c_dim), (lambda b, i: (b, 0, 0)) if ninit > 1 else (lambda b, i: (0, 0, 0))),
            _const_spec((1, d)), _const_spec((1, d)),
            _const_spec(w_in.shape), _const_spec(cw.shape), _const_spec(w_out.shape),
        ],
        out_specs=[
            pl.BlockSpec((tile, d), lambda b, i: (b * nt + i, 0)),
            pl.BlockSpec((1, 2 * shift, c_dim), lambda b, i: (b, 0, 0)),
        ],
        out_shape=[jax.ShapeDtypeStruct((rows, d), f32),
                   jax.ShapeDtypeStruct((nseq, 2 * shift, c_dim), f32)],
        scratch_shapes=[pltpu.VMEM((off + tile, c_dim), f32)],
        compiler_params=_params("arbitrary", "arbitrary"),
        name="conv_layer",
    )(x, init, pre_g, post_g, w_in, cw, w_out)


def _latent_kernel(*refs, kv_lora, with_kv):
    if with_kv:
        (x_ref, g_ref, w_dkv_ref, lat_g_ref, cos_ref, s1_ref, s2_ref, w_uk_ref, w_uv_ref,
         c_ref, kr_ref, k_ref, v_ref) = refs
    else:
        x_ref, g_ref, w_dkv_ref, lat_g_ref, cos_ref, s1_ref, s2_ref, c_ref, kr_ref = refs
    xn = _rms(x_ref[...], g_ref[...]).astype(bf16)
    ckr = _dot(xn, w_dkv_ref[...])
    c = _rms(ckr[:, :kv_lora], lat_g_ref[...])
    krb = _rope_block(ckr[:, kv_lora:kv_lora + LANES], cos_ref[...], s1_ref[...], s2_ref[...])
    c_ref[...] = c
    kr_ref[...] = krb[:, :QK_ROPE]
    if with_kv:
        cb = c.astype(bf16)
        kn = _dot(cb, w_uk_ref[...])
        for h in range(N_HEADS):
            k_ref[:, h * LANES:(h + 1) * LANES] = (kn[:, h * LANES:(h + 1) * LANES] + krb).astype(bf16)
        lane = lax.broadcasted_iota(jnp.int32, (1, N_HEADS * LANES), 1)
        ones_lane = jnp.where(lane % LANES == V_HEAD, 1.0, 0.0).astype(f32)
        v_ref[...] = (_dot(cb, w_uv_ref[...]) + ones_lane).astype(bf16)


def _latent(x, g, w_dkv_pad, lat_g, tabs, w_uk_pad=None, w_uv=None, *, nseq, tile):
    rows, d = x.shape
    kv_lora = lat_g.shape[1]
    nt = rows // nseq // tile
    with_kv = w_uk_pad is not None
    row_spec = lambda w: pl.BlockSpec((tile, w), lambda b, i: (b * nt + i, 0))
    tab_spec = pl.BlockSpec((tile, LANES), lambda b, i: (i, 0))
    in_specs = [row_spec(d), _const_spec((1, d)), _const_spec(w_dkv_pad.shape), _const_spec((1, kv_lora)),
                tab_spec, tab_spec, tab_spec]
    args = [x, g, w_dkv_pad, lat_g, *tabs]
    out_specs = [row_spec(kv_lora), row_spec(QK_ROPE)]
    out_shape = [jax.ShapeDtypeStruct((rows, kv_lora), f32), jax.ShapeDtypeStruct((rows, QK_ROPE), f32)]
    if with_kv:
        in_specs += [_const_spec(w_uk_pad.shape), _const_spec(w_uv.shape)]
        args += [w_uk_pad, w_uv]
        out_specs += [row_spec(N_HEADS * LANES), row_spec(N_HEADS * LANES)]
        out_shape += [jax.ShapeDtypeStruct((rows, N_HEADS * LANES), bf16),
                      jax.ShapeDtypeStruct((rows, N_HEADS * LANES), bf16)]
    return pl.pallas_call(
        functools.partial(_latent_kernel, kv_lora=kv_lora, with_kv=with_kv),
        grid=(nseq, nt), in_specs=in_specs, out_specs=out_specs, out_shape=out_shape,
        compiler_params=_params("arbitrary", "arbitrary"),
        name="latent_kv" if with_kv else "latent",
    )(*args)


def _mla_query_kernel(*refs, q_lora, absorbed, rows):
    if absorbed:
        (x_ref, pre_g_ref, w_in_ref, qg_ref, w_uq_ref, cos_ref, s1_ref, s2_ref, w_ukt_ref,
         ql_ref, qr_ref, z_ref) = refs
    else:
        x_ref, pre_g_ref, w_in_ref, qg_ref, w_uq_ref, cos_ref, s1_ref, s2_ref, q_ref, z_ref = refs
    h = _rms(x_ref[...], pre_g_ref[...]).astype(bf16)
    q_lat = _dot(h, w_in_ref[:, :q_lora])
    z_ref[...] = _dot(h, w_in_ref[:, q_lora:])
    qn = _rms(q_lat, qg_ref[...]).astype(bf16)
    q = _dot(qn, w_uq_ref[...])
    cos, s1, s2 = cos_ref[...], s1_ref[...], s2_ref[...]
    for hd in range(N_HEADS):
        blk = _rope_block(q[:, hd * LANES:(hd + 1) * LANES], cos, s1, s2)
        if absorbed:
            ql = _dot(blk.astype(bf16), w_ukt_ref[hd])
            for c in range(ql.shape[1] // LANES):
                ql_ref[c, pl.ds(hd, rows, stride=N_HEADS), :] = ql[:, c * LANES:(c + 1) * LANES]
            qr_ref[pl.ds(hd, rows, stride=N_HEADS), :] = blk
        else:
            q_ref[:, hd * LANES:(hd + 1) * LANES] = blk.astype(bf16)


def _mla_query(x, pre_g, w_in, qg, w_uq_pad, tabs, w_ukt_pad=None, *, nseq, tile):
    rows, d = x.shape
    q_lora = qg.shape[1]
    z_dim = w_in.shape[1] - q_lora
    nt = rows // nseq // tile
    absorbed = w_ukt_pad is not None
    row_spec = lambda w: pl.BlockSpec((tile, w), lambda b, i: (b * nt + i, 0))
    tab_spec = pl.BlockSpec((tile, LANES), lambda b, i: (i, 0))
    in_specs = [row_spec(d), _const_spec((1, d)), _const_spec(w_in.shape), _const_spec((1, q_lora)),
                _const_spec(w_uq_pad.shape), tab_spec, tab_spec, tab_spec]
    args = [x, pre_g, w_in, qg, w_uq_pad, *tabs]
    if absorbed:
        assert nseq == 1 and nt == 1
        kv_lora = w_ukt_pad.shape[2]
        in_specs.append(_const_spec(w_ukt_pad.shape))
        args.append(w_ukt_pad)
        ql_shape = (kv_lora // LANES, rows * N_HEADS, LANES)
        out_specs = [_const_spec(ql_shape), _const_spec((rows * N_HEADS, LANES)), row_spec(z_dim)]
        out_shape = [jax.ShapeDtypeStruct(ql_shape, f32),
                     jax.ShapeDtypeStruct((rows * N_HEADS, LANES), f32),
                     jax.ShapeDtypeStruct((rows, z_dim), f32)]
    else:
        out_specs = [row_spec(N_HEADS * LANES), row_spec(z_dim)]
        out_shape = [jax.ShapeDtypeStruct((rows, N_HEADS * LANES), bf16),
                     jax.ShapeDtypeStruct((rows, z_dim), f32)]
    return pl.pallas_call(
        functools.partial(_mla_query_kernel, q_lora=q_lora, absorbed=absorbed, rows=rows),
        grid=(nseq, nt), in_specs=in_specs, out_specs=out_specs, out_shape=out_shape,
        compiler_params=_params("arbitrary", "arbitrary"),
        name="mla_query_absorbed" if absorbed else "mla_query",
    )(*args)


def _prompt_attn_kernel(q_ref, k_ref, v_ref, mk_ref, mv_ref, o_ref, *, tile):
    i = pl.program_id(2)
    row = lax.broadcasted_iota(jnp.int32, (tile, tile), 0)
    col = lax.broadcasted_iota(jnp.int32, (tile, tile), 1)
    causal = col <= row
    neg = jnp.finfo(f32).min
    heads = range(q_ref.shape[2] // LANES)
    qs = [q_ref[0, :, hh * LANES:(hh + 1) * LANES] for hh in heads]

    def scores(hh, start, size):
        return _dot_nt(qs[hh], k_ref[0, pl.ds(start, size), hh * LANES:(hh + 1) * LANES])

    def values(hh, start, size):
        return v_ref[0, pl.ds(start, size), hh * LANES:(hh + 1) * LANES]

    def update(m, acc, s, v):
        m_new = jnp.maximum(m, jnp.max(s, axis=-1, keepdims=True))
        alpha = jnp.exp2(m - m_new)
        p = jnp.exp2(s - m_new)
        return m_new, alpha * acc + _dot(p.astype(bf16), v)

    d0 = pl.multiple_of(i * tile, tile)
    carry = []
    for hh in heads:
        s = jnp.where(causal, scores(hh, d0, tile), neg)
        m = jnp.max(s, axis=-1, keepdims=True)
        carry += [m, _dot(jnp.exp2(s - m).astype(bf16), values(hh, d0, tile))]

    def step(carry, start, size):
        out = []
        for hh in heads:
            out += update(carry[2 * hh], carry[2 * hh + 1], scores(hh, start, size), values(hh, start, size))
        return tuple(out)

    carry = lax.fori_loop(0, i // 2, lambda j, c: step(c, pl.multiple_of(j * 2 * tile, 2 * tile), 2 * tile),
                          tuple(carry))
    carry = lax.cond(i % 2 == 1, lambda c: step(c, pl.multiple_of((i - 1) * tile, tile), tile), lambda c: c, carry)
    outs = []
    for hh in heads:
        s_meta = _dot_nt(qs[hh], mk_ref[:, hh * LANES:(hh + 1) * LANES])
        _, acc = update(carry[2 * hh], carry[2 * hh + 1], s_meta, mv_ref[:, hh * LANES:(hh + 1) * LANES])
        outs.append(acc / acc[:, V_HEAD:V_HEAD + 1])
    lane = lax.broadcasted_iota(jnp.int32, (tile, LANES), 1)
    for g in range(len(outs) // 2):
        o_ref[0, :, g * LANES:(g + 1) * LANES] = jnp.where(lane < V_HEAD, outs[2 * g],
                                                          pltpu.roll(outs[2 * g + 1], V_HEAD, 1))


def _prompt_attn(q, k, v, mk, mv, *, tile):
    b, t, _ = q.shape
    nq = t // tile
    hps = ATTN_HEADS_PER_STEP
    return pl.pallas_call(
        functools.partial(_prompt_attn_kernel, tile=tile),
        grid=(b, N_HEADS // hps, nq),
        in_specs=[
            pl.BlockSpec((1, tile, hps * LANES), lambda b, g, i: (b, i, g)),
            pl.BlockSpec((1, t, hps * LANES), lambda b, g, i: (b, 0, g)),
            pl.BlockSpec((1, t, hps * LANES), lambda b, g, i: (b, 0, g)),
            pl.BlockSpec((N_META, hps * LANES), lambda b, g, i: (0, g)),
            pl.BlockSpec((N_META, hps * LANES), lambda b, g, i: (0, g)),
        ],
        out_specs=pl.BlockSpec((1, tile, hps * V_HEAD), lambda b, g, i: (b, i, g)),
        out_shape=jax.ShapeDtypeStruct((b, t, N_HEADS * V_HEAD), f32),
        compiler_params=_params("arbitrary", "arbitrary", "arbitrary"),
        name="prompt_attn",
    )(q, k, v, mk, mv)


def _sample_attn_kernel(pt_ref, ql_ref, qr_ref, cn_ref, krn_ref, ckv_hbm, krt_hbm, o_ref,
                        kv_land, kr_land, kb16, kr16, sem_kv, sem_kr, *, n_pages, page):
    b = pl.program_id(0)
    slot = lax.rem(b, 2)

    def page_copies(req, slot_, p):
        pg = pt_ref[req * n_pages + p]
        return (pltpu.make_async_copy(ckv_hbm.at[pg], kv_land.at[slot_, p], sem_kv.at[slot_]),
                pltpu.make_async_copy(krt_hbm.at[pg], kr_land.at[slot_, p], sem_kr.at[slot_]))

    def start_request(req, slot_):
        def body(p, c):
            for cp in page_copies(req, slot_, p):
                cp.start()
            return c
        lax.fori_loop(0, n_pages, body, 0)

    @pl.when(b == 0)
    def _():
        start_request(0, 0)

    @pl.when(b + 1 < pl.num_programs(0))
    def _():
        start_request(b + 1, 1 - slot)

    def wait_body(p, c):
        for cp in page_copies(b, slot, p):
            cp.wait()
        return c
    lax.fori_loop(0, n_pages, wait_body, 0)

    for p in range(n_pages):
        kb16[p * page:(p + 1) * page, :] = kv_land[slot, p].astype(bf16)
        kr16[:, p * page:(p + 1) * page] = kr_land[slot, p].astype(bf16)

    n_half = ql_ref.shape[0]
    ql = jnp.concatenate([ql_ref[c] for c in range(n_half)], axis=-1).astype(bf16)
    qr = qr_ref[:, :QK_ROPE].astype(bf16)
    nq = ql.shape[0]
    kb = kb16[...]
    s = _dot_nt(ql, kb) + _dot(qr, kr16[...])
    cn = cn_ref[0].astype(bf16)
    krn = krn_ref[0].astype(bf16)
    npad = cn.shape[0]
    s_new = _dot_nt(ql, cn) + _dot_nt(qr, krn)
    row = lax.broadcasted_iota(jnp.int32, (nq, npad), 0)
    col = lax.broadcasted_iota(jnp.int32, (nq, npad), 1)
    s_new = jnp.where(col * N_HEADS <= row, s_new, jnp.finfo(f32).min)
    m = jnp.maximum(jnp.max(s, axis=-1, keepdims=True), jnp.max(s_new, axis=-1, keepdims=True))
    p = jnp.exp2(s - m)
    p_new = jnp.exp2(s_new - m)
    l = jnp.sum(p, axis=-1, keepdims=True) + jnp.sum(p_new, axis=-1, keepdims=True)
    o = (_dot(p.astype(bf16), kb) + _dot(p_new.astype(bf16), cn)) / l
    for c in range(n_half):
        o_ref[c] = o[:, c * LANES:(c + 1) * LANES]


def _sample_attn(page_table, q_lat, q_rope, c_new, kr_new, cache_ckv, cache_krope_t):
    nreq, n_pages = page_table.shape
    _, page, kv_lora = cache_ckv.shape
    n_half = q_lat.shape[0]
    nq = q_lat.shape[1] // nreq
    lpad = c_new.shape[1]
    grid_spec = pltpu.PrefetchScalarGridSpec(
        num_scalar_prefetch=1, grid=(nreq,),
        in_specs=[
            pl.BlockSpec((n_half, nq, LANES), lambda b, pt: (0, b, 0)),
            pl.BlockSpec((nq, LANES), lambda b, pt: (b, 0)),
            pl.BlockSpec((1, lpad, kv_lora), lambda b, pt: (b, 0, 0)),
            pl.BlockSpec((1, lpad, QK_ROPE), lambda b, pt: (b, 0, 0)),
            pl.BlockSpec(memory_space=pl.ANY),
            pl.BlockSpec(memory_space=pl.ANY),
        ],
        out_specs=pl.BlockSpec((n_half, nq, LANES), lambda b, pt: (0, b, 0)),
        scratch_shapes=[
            pltpu.VMEM((2, n_pages, page, kv_lora), f32),
            pltpu.VMEM((2, n_pages, QK_ROPE, page), f32),
            pltpu.VMEM((n_pages * page, kv_lora), bf16),
            pltpu.VMEM((QK_ROPE, n_pages * page), bf16),
            pltpu.SemaphoreType.DMA((2,)),
            pltpu.SemaphoreType.DMA((2,)),
        ],
    )
    return pl.pallas_call(
        functools.partial(_sample_attn_kernel, n_pages=n_pages, page=page),
        grid_spec=grid_spec,
        out_shape=jax.ShapeDtypeStruct((n_half, nreq * nq, LANES), f32),
        compiler_params=_params("arbitrary"),
        name="sample_attn",
    )(page_table.reshape(-1), q_lat, q_rope, c_new, kr_new, cache_ckv, cache_krope_t)


def _mla_out_kernel(*refs, absorbed, rows):
    if absorbed:
        ol_ref, w_uv_ref, z_ref, x_ref, w_out_ref, post_g_ref, xo_ref = refs
        def head_rows(hd):
            halves = [ol_ref[c, pl.ds(hd, rows, stride=N_HEADS), :] for c in range(ol_ref.shape[0])]
            return jnp.concatenate(halves, axis=-1).astype(bf16)

        parts = []
        for g in range(N_HEADS // 2):
            parts.append(_dot(head_rows(2 * g), w_uv_ref[2 * g]) + _dot(head_rows(2 * g + 1), w_uv_ref[2 * g + 1]))
        o = jnp.concatenate(parts, axis=-1)
    else:
        o_ref, z_ref, x_ref, w_out_ref, post_g_ref, xo_ref = refs
        o = o_ref[...]
    g = (o * jax.nn.silu(z_ref[...])).astype(bf16)
    m = _dot(g, w_out_ref[...])
    xo_ref[...] = x_ref[...] + _rms(m, post_g_ref[...])


def _mla_out(o, z, x, w_out, post_g, w_uv_pad=None, *, tile):
    rows, d = x.shape
    z_dim = z.shape[1]
    nt = rows // tile
    absorbed = w_uv_pad is not None
    row_spec = lambda w: pl.BlockSpec((tile, w), lambda i: (i, 0))
    if absorbed:
        assert nt == 1
        in_specs = [_const_spec(o.shape), _const_spec(w_uv_pad.shape)]
        args = [o, w_uv_pad]
    else:
        in_specs = [row_spec(z_dim)]
        args = [o]
    in_specs += [row_spec(z_dim), row_spec(d), _const_spec(w_out.shape), _const_spec((1, d))]
    args += [z, x, w_out, post_g]
    return pl.pallas_call(
        functools.partial(_mla_out_kernel, absorbed=absorbed, rows=rows),
        grid=(nt,), in_specs=in_specs, out_specs=row_spec(d),
        out_shape=jax.ShapeDtypeStruct((rows, d), f32),
        compiler_params=_params("arbitrary"),
        name="mla_out_absorbed" if absorbed else "mla_out",
    )(*args)


def _rope_tables(pos, scale):
    inv = ROPE_THETA ** (-jnp.arange(HALF_ROPE, dtype=f32) / HALF_ROPE)
    ang = pos.astype(f32)[:, None] * inv[None, :]
    cos, sin = jnp.cos(ang), jnp.sin(ang)
    r = pos.shape[0]
    zeros = lambda w: jnp.zeros((r, w), f32)
    c = jnp.concatenate([cos, cos, zeros(LANES - QK_ROPE - QK_NOPE), jnp.ones((r, QK_NOPE), f32)], axis=-1)
    s1 = jnp.concatenate([-sin, zeros(LANES - HALF_ROPE)], axis=-1)
    s2 = jnp.concatenate([zeros(HALF_ROPE), sin, zeros(LANES - QK_ROPE)], axis=-1)
    return c * scale, s1 * scale, s2 * scale


def _head_blocks(w_rope, w_nope):
    pad = jnp.zeros(w_nope.shape[:-1] + (LANES - QK_ROPE - QK_NOPE,), w_nope.dtype)
    blk = jnp.concatenate([w_rope, pad, w_nope], axis=-1)
    return blk.reshape(blk.shape[:-2] + (N_HEADS * LANES,))


def kernel(x_prompt, x_sample, cache_ckv, cache_krope, state_conv, page_table, meta_tokens,
           pre_norm_g, post_norm_g, w_in_conv, conv_w, w_out_conv, kv_norm_g, w_dkv,
           kv_lat_norm_g, w_uk, w_uv, w_in_mla, q_norm_g, w_uq, w_out_mla):
    bp, seq, d = x_prompt.shape
    bs, ls, _ = x_sample.shape
    n_a = w_in_conv.shape[0]
    n_b = w_in_mla.shape[0]
    c_dim = conv_w.shape[2]
    kv_lora = kv_lat_norm_g.shape[0]
    q_lora = q_norm_g.shape[1]
    past_len = page_table.shape[1] * cache_ckv.shape[1]

    row = lambda v: v.reshape(1, -1).astype(f32)
    w_in_conv_b = w_in_conv.astype(bf16)
    w_out_conv_b = w_out_conv.astype(bf16)
    w_in_mla_b = w_in_mla.astype(bf16)
    w_out_mla_b = w_out_mla.astype(bf16)
    w_dkv_pad = jnp.concatenate([w_dkv, jnp.zeros((d, LANES - QK_ROPE), w_dkv.dtype)], axis=-1).astype(bf16)
    uq = w_uq.reshape(n_b, q_lora, N_HEADS, QK_NOPE + QK_ROPE)
    w_uq_pad = _head_blocks(uq[..., QK_NOPE:], uq[..., :QK_NOPE]).astype(bf16)
    w_uk_blk = _head_blocks(jnp.zeros((kv_lora, N_HEADS, QK_ROPE), w_uk.dtype), w_uk)
    w_uk_pad = w_uk_blk.astype(bf16)
    w_ukt_pad = jnp.transpose(w_uk_blk.reshape(kv_lora, N_HEADS, LANES), (1, 2, 0)).astype(bf16)
    w_uv_blk = jnp.concatenate([w_uv, jnp.zeros((kv_lora, N_HEADS, LANES - V_HEAD), w_uv.dtype)], axis=-1)
    w_uv_blk = w_uv_blk.reshape(kv_lora, N_HEADS * LANES).astype(bf16)
    uv = jnp.transpose(w_uv, (1, 0, 2))
    zv = jnp.zeros_like(uv)
    even = (jnp.arange(N_HEADS) % 2 == 0)[:, None, None]
    w_uv_pad = jnp.concatenate([jnp.where(even, uv, zv), jnp.where(even, zv, uv)], axis=-1).astype(bf16)

    def trunk_a(x, inits, *, nseq, tile, shift):
        states = []
        for l in range(n_a):
            x, st = _conv_layer(x, inits[l], row(pre_norm_g[l]), row(post_norm_g[l]), w_in_conv_b[l],
                                conv_w[l], w_out_conv_b[l], nseq=nseq, tile=tile, shift=shift)
            states.append(st)
        return x, states

    zero_init = jnp.zeros((1, SUBLANES, c_dim), f32)
    xm, meta_states = trunk_a(meta_tokens.astype(f32), [zero_init] * n_a, nseq=1, tile=N_META, shift=1)
    tabs_meta = _rope_tables(jnp.arange(N_META, dtype=jnp.int32), 1.0)
    c_meta, kr_meta, k_meta, v_meta = _latent(xm, row(kv_norm_g), w_dkv_pad, row(kv_lat_norm_g), tabs_meta,
                                              w_uk_pad, w_uv_blk, nseq=1, tile=N_META)

    tile_p = min(ROW_TILE, seq)
    inits_p = [jnp.concatenate([jnp.zeros((1, SUBLANES - 2, c_dim), f32), st], axis=1) for st in meta_states]
    xp, prompt_states = trunk_a(x_prompt.reshape(bp * seq, d), inits_p, nseq=bp, tile=tile_p, shift=1)
    pos_p = N_META + jnp.arange(seq, dtype=jnp.int32)
    c_p, kr_p, k_p, v_p = _latent(xp, row(kv_norm_g), w_dkv_pad, row(kv_lat_norm_g), _rope_tables(pos_p, 1.0),
                                  w_uk_pad, w_uv_blk, nseq=bp, tile=tile_p)
    tabs_q = _rope_tables(pos_p, SOFTMAX_SCALE * LOG2_E)
    attn_tile = min(ATTN_TILE, seq)
    for j in range(n_b):
        l = n_a + j
        q, z = _mla_query(xp, row(pre_norm_g[l]), w_in_mla_b[j], row(q_norm_g[j]), w_uq_pad[j], tabs_q,
                          nseq=bp, tile=tile_p)
        o = _prompt_attn(q.reshape(bp, seq, -1), k_p.reshape(bp, seq, -1), v_p.reshape(bp, seq, -1),
                         k_meta, v_meta, tile=attn_tile)
        xp = _mla_out(o.reshape(bp * seq, -1), z, xp, w_out_mla_b[j], row(post_norm_g[l]), tile=tile_p)
    y_prompt = xp.reshape(bp, seq, d)
    bcast = lambda a: jnp.broadcast_to(a[None], (bp,) + a.shape)
    ckv_prompt = jnp.concatenate([bcast(c_meta), c_p.reshape(bp, seq, kv_lora)], axis=1)
    krope_prompt = jnp.concatenate([bcast(kr_meta), kr_p.reshape(bp, seq, QK_ROPE)], axis=1)
    conv_prompt = jnp.stack([st for st in prompt_states])

    rs = bs * ls
    xs = jnp.transpose(x_sample, (1, 0, 2)).reshape(rs, d)
    inits_s = [jnp.transpose(state_conv[l], (1, 0, 2)).reshape(1, (CONV_WIDTH - 1) * bs, c_dim) for l in range(n_a)]
    xs, sample_states = trunk_a(xs, inits_s, nseq=1, tile=rs, shift=bs)
    conv_sample = jnp.stack([jnp.transpose(st.reshape(CONV_WIDTH - 1, bs, c_dim), (1, 0, 2)) for st in sample_states])
    xs = jnp.transpose(xs.reshape(ls, bs, d), (1, 0, 2)).reshape(rs, d)
    pos_s = jnp.tile(past_len + jnp.arange(ls, dtype=jnp.int32), bs)
    c_s, kr_s = _latent(xs, row(kv_norm_g), w_dkv_pad, row(kv_lat_norm_g), _rope_tables(pos_s, 1.0),
                        nseq=1, tile=rs)
    lpad = 16
    c_new = jnp.pad(c_s.reshape(bs, ls, kv_lora), ((0, 0), (0, lpad - ls), (0, 0)))
    kr_new = jnp.pad(kr_s.reshape(bs, ls, QK_ROPE), ((0, 0), (0, lpad - ls), (0, 0)))
    tabs_qs = _rope_tables(pos_s, SOFTMAX_SCALE * LOG2_E)
    cache_krope_t = jnp.swapaxes(cache_krope, 1, 2)
    for j in range(n_b):
        l = n_a + j
        q_lat, q_rope, z = _mla_query(xs, row(pre_norm_g[l]), w_in_mla_b[j], row(q_norm_g[j]), w_uq_pad[j],
                                      tabs_qs, w_ukt_pad, nseq=1, tile=rs)
        o_lat = _sample_attn(page_table, q_lat, q_rope, c_new, kr_new, cache_ckv, cache_krope_t)
        xs = _mla_out(o_lat, z, xs, w_out_mla_b[j], row(post_norm_g[l]), w_uv_pad, tile=rs)
    y_sample = xs.reshape(bs, ls, d)
    ckv_sample = c_s.reshape(bs, ls, kv_lora)
    krope_sample = kr_s.reshape(bs, ls, QK_ROPE)

    return (y_prompt, y_sample, ckv_prompt, krope_prompt, conv_prompt, ckv_sample, krope_sample, conv_sample)
```

```python
import functools

import jax
import jax.numpy as jnp
from jax import lax
from jax.experimental import pallas as pl
from jax.experimental.pallas import tpu as pltpu

N_META = 16
N_HEADS = 16
QK_NOPE = 64
QK_ROPE = 32
V_HEAD = 64
ROPE_THETA = 10000.0
RMS_EPS = 1e-6
CONV_WIDTH = 3
SOFTMAX_SCALE = (QK_NOPE + QK_ROPE) ** -0.5
LOG2_E = 1.4426950408889634

LANES = 128
SUBLANES = 8
HALF_ROPE = QK_ROPE // 2
VMEM_LIMIT = 56 * 1024 * 1024

ROW_TILE = 512
ATTN_TILE = 1024
ATTN_HEADS_PER_STEP = 4
SAMPLE_ATTN_CHUNKS = 4

bf16 = jnp.bfloat16
f32 = jnp.float32


def _rms(x, g):
    return x * lax.rsqrt(jnp.mean(x * x, axis=-1, keepdims=True) + RMS_EPS) * g


def _dot(a, b):
    return jnp.dot(a, b, preferred_element_type=f32)


def _dot_nt(a, b):
    return lax.dot_general(a, b, (((1,), (1,)), ((), ())), preferred_element_type=f32)


def _rope_block(blk, cos, s1, s2):
    return (blk * cos + pltpu.roll(blk, LANES - HALF_ROPE, 1) * s1 + pltpu.roll(blk, HALF_ROPE, 1) * s2)


def _params(*sem):
    return pltpu.CompilerParams(dimension_semantics=sem, vmem_limit_bytes=VMEM_LIMIT)


def _const_spec(shape):
    nd = len(shape)
    return pl.BlockSpec(shape, lambda *_: (0,) * nd)


def _conv_layer_kernel(x_ref, init_ref, pre_g_ref, post_g_ref, w_in_ref, cw_ref, w_out_ref,
                       xo_ref, st_ref, vbuf, *, tile, off, shift, c_dim):
    i = pl.program_id(1)

    @pl.when(i == 0)
    def _():
        vbuf[0:off, :] = init_ref[0]

    x = x_ref[...]
    h = _rms(x, pre_g_ref[...]).astype(bf16)

    def proj(k):
        return _dot(h, w_in_ref[:, k * c_dim:(k + 1) * c_dim])

    vbuf[off:off + tile, :] = proj(1) * proj(2)
    cw = cw_ref[...]
    y = cw[0:1] * vbuf[off - 2 * shift:off - 2 * shift + tile, :]
    y = y + cw[1:2] * vbuf[off - shift:off - shift + tile, :]
    y = y + cw[2:3] * vbuf[off:off + tile, :]
    z = proj(3)
    g = (proj(0) * y * jax.nn.silu(z)).astype(bf16)
    m = _dot(g, w_out_ref[...])
    xo_ref[...] = x + _rms(m, post_g_ref[...])

    @pl.when(i == pl.num_programs(1) - 1)
    def _():
        st_ref[0] = vbuf[off + tile - 2 * shift:off + tile, :]

    vbuf[0:off, :] = vbuf[tile:tile + off, :]


def _conv_layer(x, init, pre_g, post_g, w_in, cw, w_out, *, nseq, tile, shift):
    rows, d = x.shape
    c_dim = cw.shape[1]
    t = rows // nseq
    nt = t // tile
    off = init.shape[1]
    ninit = init.shape[0]
    kern = functools.partial(_conv_layer_kernel, tile=tile, off=off, shift=shift, c_dim=c_dim)
    return pl.pallas_call(
        kern,
        grid=(nseq, nt),
        in_specs=[
            pl.BlockSpec((tile, d), lambda b, i: (b * nt + i, 0)),
            pl.BlockSpec((1, off, c_dim), (lambda b, i: (b, 0, 0)) if ninit > 1 else (lambda b, i: (0, 0, 0))),
            _const_spec((1, d)), _const_spec((1, d)),
            _const_spec(w_in.shape), _const_spec(cw.shape), _const_spec(w_out.shape),
        ],
        out_specs=[
            pl.BlockSpec((tile, d), lambda b, i: (b * nt + i, 0)),
            pl.BlockSpec((1, 2 * shift, c_dim), lambda b, i: (b, 0, 0)),
        ],
        out_shape=[jax.ShapeDtypeStruct((rows, d), f32),
                   jax.ShapeDtypeStruct((nseq, 2 * shift, c_dim), f32)],
        scratch_shapes=[pltpu.VMEM((off + tile, c_dim), f32)],
        compiler_params=_params("arbitrary", "arbitrary"),
        name="conv_layer",
    )(x, init, pre_g, post_g, w_in, cw, w_out)


def _latent_kernel(*refs, kv_lora, with_kv):
    if with_kv:
        (x_ref, g_ref, w_dkv_ref, lat_g_ref, cos_ref, s1_ref, s2_ref, w_uk_ref, w_uv_ref,
         c_ref, kr_ref, k_ref, v_ref) = refs
    else:
        x_ref, g_ref, w_dkv_ref, lat_g_ref, cos_ref, s1_ref, s2_ref, c_ref, kr_ref = refs
    xn = _rms(x_ref[...], g_ref[...]).astype(bf16)
    ckr = _dot(xn, w_dkv_ref[...])
    c = _rms(ckr[:, :kv_lora], lat_g_ref[...])
    krb = _rope_block(ckr[:, kv_lora:kv_lora + LANES], cos_ref[...], s1_ref[...], s2_ref[...])
    c_ref[...] = c
    kr_ref[...] = krb[:, :QK_ROPE]
    if with_kv:
        cb = c.astype(bf16)
        kn = _dot(cb, w_uk_ref[...])
        for h in range(N_HEADS):
            k_ref[:, h * LANES:(h + 1) * LANES] = (kn[:, h * LANES:(h + 1) * LANES] + krb).astype(bf16)
        lane = lax.broadcasted_iota(jnp.int32, (1, N_HEADS * LANES), 1)
        ones_lane = jnp.where(lane % LANES == V_HEAD, 1.0, 0.0).astype(f32)
        v_ref[...] = (_dot(cb, w_uv_ref[...]) + ones_lane).astype(bf16)


def _latent(x, g, w_dkv_pad, lat_g, tabs, w_uk_pad=None, w_uv=None, *, nseq, tile):
    rows, d = x.shape
    kv_lora = lat_g.shape[1]
    nt = rows // nseq // tile
    with_kv = w_uk_pad is not None
    row_spec = lambda w: pl.BlockSpec((tile, w), lambda b, i: (b * nt + i, 0))
    tab_spec = pl.BlockSpec((tile, LANES), lambda b, i: (i, 0))
    in_specs = [row_spec(d), _const_spec((1, d)), _const_spec(w_dkv_pad.shape), _const_spec((1, kv_lora)),
                tab_spec, tab_spec, tab_spec]
    args = [x, g, w_dkv_pad, lat_g, *tabs]
    out_specs = [row_spec(kv_lora), row_spec(QK_ROPE)]
    out_shape = [jax.ShapeDtypeStruct((rows, kv_lora), f32), jax.ShapeDtypeStruct((rows, QK_ROPE), f32)]
    if with_kv:
        in_specs += [_const_spec(w_uk_pad.shape), _const_spec(w_uv.shape)]
        args += [w_uk_pad, w_uv]
        out_specs += [row_spec(N_HEADS * LANES), row_spec(N_HEADS * LANES)]
        out_shape += [jax.ShapeDtypeStruct((rows, N_HEADS * LANES), bf16),
                      jax.ShapeDtypeStruct((rows, N_HEADS * LANES), bf16)]
    return pl.pallas_call(
        functools.partial(_latent_kernel, kv_lora=kv_lora, with_kv=with_kv),
        grid=(nseq, nt), in_specs=in_specs, out_specs=out_specs, out_shape=out_shape,
        compiler_params=_params("arbitrary", "arbitrary"),
        name="latent_kv" if with_kv else "latent",
    )(*args)


def _mla_query_kernel(*refs, q_lora, absorbed, rows):
    if absorbed:
        (x_ref, pre_g_ref, w_in_ref, qg_ref, w_uq_ref, cos_ref, s1_ref, s2_ref, w_ukt_ref,
         ql_ref, qr_ref, z_ref) = refs
    else:
        x_ref, pre_g_ref, w_in_ref, qg_ref, w_uq_ref, cos_ref, s1_ref, s2_ref, q_ref, z_ref = refs
    h = _rms(x_ref[...], pre_g_ref[...]).astype(bf16)
    q_lat = _dot(h, w_in_ref[:, :q_lora])
    z_ref[...] = _dot(h, w_in_ref[:, q_lora:])
    qn = _rms(q_lat, qg_ref[...]).astype(bf16)
    q = _dot(qn, w_uq_ref[...])
    cos, s1, s2 = cos_ref[...], s1_ref[...], s2_ref[...]
    for hd in range(N_HEADS):
        blk = _rope_block(q[:, hd * LANES:(hd + 1) * LANES], cos, s1, s2)
        if absorbed:
            ql = _dot(blk.astype(bf16), w_ukt_ref[hd])
            for c in range(ql.shape[1] // LANES):
                ql_ref[c, pl.ds(hd, rows, stride=N_HEADS), :] = ql[:, c * LANES:(c + 1) * LANES]
            qr_ref[pl.ds(hd, rows, stride=N_HEADS), :] = blk
        else:
            q_ref[:, hd * LANES:(hd + 1) * LANES] = blk.astype(bf16)


def _mla_query(x, pre_g, w_in, qg, w_uq_pad, tabs, w_ukt_pad=None, *, nseq, tile):
    rows, d = x.shape
    q_lora = qg.shape[1]
    z_dim = w_in.shape[1] - q_lora
    nt = rows // nseq // tile
    absorbed = w_ukt_pad is not None
    row_spec = lambda w: pl.BlockSpec((tile, w), lambda b, i: (b * nt + i, 0))
    tab_spec = pl.BlockSpec((tile, LANES), lambda b, i: (i, 0))
    in_specs = [row_spec(d), _const_spec((1, d)), _const_spec(w_in.shape), _const_spec((1, q_lora)),
                _const_spec(w_uq_pad.shape), tab_spec, tab_spec, tab_spec]
    args = [x, pre_g, w_in, qg, w_uq_pad, *tabs]
    if absorbed:
        assert nseq == 1 and nt == 1
        kv_lora = w_ukt_pad.shape[2]
        in_specs.append(_const_spec(w_ukt_pad.shape))
        args.append(w_ukt_pad)
        ql_shape = (kv_lora // LANES, rows * N_HEADS, LANES)
        out_specs = [_const_spec(ql_shape), _const_spec((rows * N_HEADS, LANES)), row_spec(z_dim)]
        out_shape = [jax.ShapeDtypeStruct(ql_shape, f32),
                     jax.ShapeDtypeStruct((rows * N_HEADS, LANES), f32),
                     jax.ShapeDtypeStruct((rows, z_dim), f32)]
    else:
        out_specs = [row_spec(N_HEADS * LANES), row_spec(z_dim)]
        out_shape = [jax.ShapeDtypeStruct((rows, N_HEADS * LANES), bf16),
                     jax.ShapeDtypeStruct((rows, z_dim), f32)]
    return pl.pallas_call(
        functools.partial(_mla_query_kernel, q_lora=q_lora, absorbed=absorbed, rows=rows),
        grid=(nseq, nt), in_specs=in_specs, out_specs=out_specs, out_shape=out_shape,
        compiler_params=_params("arbitrary", "arbitrary"),
        name="mla_query_absorbed" if absorbed else "mla_query",
    )(*args)


def _prompt_attn_kernel(q_ref, k_ref, v_ref, mk_ref, mv_ref, o_ref, *, tile):
    i = pl.program_id(2)
    neg = jnp.finfo(f32).min
    heads = range(q_ref.shape[2] // LANES)
    n_sub = 2
    sub = tile // n_sub

    def keys(hh, start, size):
        return k_ref[0, pl.ds(start, size), hh * LANES:(hh + 1) * LANES]

    def values(hh, start, size):
        return v_ref[0, pl.ds(start, size), hh * LANES:(hh + 1) * LANES]

    d0 = pl.multiple_of(i * tile, tile)

    def init_rows(hh, r):
        q = q_ref[0, r * sub:(r + 1) * sub, hh * LANES:(hh + 1) * LANES]
        nk = (r + 1) * sub
        row = lax.broadcasted_iota(jnp.int32, (sub, nk), 0) + r * sub
        col = lax.broadcasted_iota(jnp.int32, (sub, nk), 1)
        s = jnp.where(col <= row, _dot_nt(q, keys(hh, d0, nk)), neg)
        s_meta = _dot_nt(q, mk_ref[:, hh * LANES:(hh + 1) * LANES])
        m = jnp.maximum(jnp.max(s, axis=-1, keepdims=True), jnp.max(s_meta, axis=-1, keepdims=True))
        acc = _dot(jnp.exp2(s - m).astype(bf16), values(hh, d0, nk))
        acc = acc + _dot(jnp.exp2(s_meta - m).astype(bf16), mv_ref[:, hh * LANES:(hh + 1) * LANES])
        return m, acc

    carry = []
    for hh in heads:
        parts = [init_rows(hh, r) for r in range(n_sub)]
        m_rows = jnp.concatenate([jnp.broadcast_to(pt[0], (sub, LANES)) for pt in parts], axis=0)
        carry += [jnp.max(m_rows, axis=-1, keepdims=True), jnp.concatenate([pt[1] for pt in parts], axis=0)]

    qs = [q_ref[0, :, hh * LANES:(hh + 1) * LANES] for hh in heads]

    def step(carry, start, size):
        out = []
        for hh in heads:
            m, acc = carry[2 * hh], carry[2 * hh + 1]
            s = _dot_nt(qs[hh], keys(hh, start, size))
            m_new = jnp.maximum(m, jnp.max(s, axis=-1, keepdims=True))
            alpha = jnp.exp2(m - m_new)
            p = jnp.exp2(s - m_new)
            out += [m_new, alpha * acc + _dot(p.astype(bf16), values(hh, start, size))]
        return tuple(out)

    carry = lax.fori_loop(0, i // 2, lambda j, c: step(c, pl.multiple_of(j * 2 * tile, 2 * tile), 2 * tile),
                          tuple(carry))
    carry = lax.cond(i % 2 == 1, lambda c: step(c, pl.multiple_of((i - 1) * tile, tile), tile), lambda c: c, carry)
    outs = [carry[2 * hh + 1] / carry[2 * hh + 1][:, V_HEAD:V_HEAD + 1] for hh in heads]
    lane = lax.broadcasted_iota(jnp.int32, (tile, LANES), 1)
    for g in range(len(outs) // 2):
        o_ref[0, :, g * LANES:(g + 1) * LANES] = jnp.where(lane < V_HEAD, outs[2 * g],
                                                          pltpu.roll(outs[2 * g + 1], V_HEAD, 1))


def _prompt_attn(q, k, v, mk, mv, *, tile):
    b, t, _ = q.shape
    nq = t // tile
    hps = ATTN_HEADS_PER_STEP
    return pl.pallas_call(
        functools.partial(_prompt_attn_kernel, tile=tile),
        grid=(b, N_HEADS // hps, nq),
        in_specs=[
            pl.BlockSpec((1, tile, hps * LANES), lambda b, g, i: (b, i, g)),
            pl.BlockSpec((1, t, hps * LANES), lambda b, g, i: (b, 0, g)),
            pl.BlockSpec((1, t, hps * LANES), lambda b, g, i: (b, 0, g)),
            pl.BlockSpec((N_META, hps * LANES), lambda b, g, i: (0, g)),
            pl.BlockSpec((N_META, hps * LANES), lambda b, g, i: (0, g)),
        ],
        out_specs=pl.BlockSpec((1, tile, hps * V_HEAD), lambda b, g, i: (b, i, g)),
        out_shape=jax.ShapeDtypeStruct((b, t, N_HEADS * V_HEAD), f32),
        compiler_params=_params("arbitrary", "arbitrary", "arbitrary"),
        name="prompt_attn",
    )(q, k, v, mk, mv)


def _sample_attn_kernel(pt_ref, ql_ref, qr_ref, cn_ref, krn_ref, ckv_hbm, krt_hbm, o_ref,
                        kv_land, kr_land, kb16, kr16, sem_kv, sem_kr, *, n_pages, page, n_chunks):
    b = pl.program_id(0)
    slot = lax.rem(b, 2)

    def page_copies(req, slot_, p):
        pg = pt_ref[req * n_pages + p]
        return (pltpu.make_async_copy(ckv_hbm.at[pg], kv_land.at[slot_, p], sem_kv.at[slot_]),
                pltpu.make_async_copy(krt_hbm.at[pg], kr_land.at[slot_, p], sem_kr.at[slot_]))

    def wait_slot(slot_):
        pltpu.make_async_copy(ckv_hbm.at[pl.ds(0, n_pages)], kv_land.at[slot_], sem_kv.at[slot_]).wait()
        pltpu.make_async_copy(krt_hbm.at[pl.ds(0, n_pages)], kr_land.at[slot_], sem_kr.at[slot_]).wait()

    @pl.when(b == 0)
    def _():
        def body(p, c):
            for cp in page_copies(0, 0, p):
                cp.start()
            return c
        lax.fori_loop(0, n_pages, body, 0)

    wait_slot(slot)
    nxt = jnp.minimum(b + 1, pl.num_programs(0) - 1)

    n_half = ql_ref.shape[0]
    ql = jnp.concatenate([ql_ref[c] for c in range(n_half)], axis=-1).astype(bf16)
    qr = qr_ref[:, :QK_ROPE].astype(bf16)
    nq = ql.shape[0]

    ppc = n_pages // n_chunks
    parts = []
    for c in range(n_chunks):
        for p in range(c * ppc, (c + 1) * ppc):
            for cp in page_copies(nxt, 1 - slot, p):
                cp.start()
            kb16[p * page:(p + 1) * page, :] = kv_land[slot, p].astype(bf16)
            kr16[:, p * page:(p + 1) * page] = kr_land[slot, p].astype(bf16)
        kb = kb16[c * ppc * page:(c + 1) * ppc * page, :]
        half = ppc * page // 2
        s = jnp.concatenate([_dot_nt(ql, kb[:half]), _dot_nt(ql, kb[half:])], axis=-1)
        s = s + _dot(qr, kr16[:, c * ppc * page:(c + 1) * ppc * page])
        m_c = jnp.max(s, axis=-1, keepdims=True)
        p_c = jnp.exp2(s - m_c)
        pb = p_c.astype(bf16)
        a_c = _dot(pb[:, :half], kb[:half]) + _dot(pb[:, half:], kb[half:])
        parts.append((m_c, jnp.sum(p_c, axis=-1, keepdims=True), a_c))
    cn = cn_ref[0].astype(bf16)
    krn = krn_ref[0].astype(bf16)
    npad = cn.shape[0]
    s_new = _dot_nt(ql, cn) + _dot_nt(qr, krn)
    row = lax.broadcasted_iota(jnp.int32, (nq, npad), 0)
    col = lax.broadcasted_iota(jnp.int32, (nq, npad), 1)
    s_new = jnp.where(col * N_HEADS <= row, s_new, jnp.finfo(f32).min)
    m_n = jnp.max(s_new, axis=-1, keepdims=True)
    p_n = jnp.exp2(s_new - m_n)
    parts.append((m_n, jnp.sum(p_n, axis=-1, keepdims=True), _dot(p_n.astype(bf16), cn)))

    m = functools.reduce(jnp.maximum, [pt_[0] for pt_ in parts])
    l = sum(jnp.exp2(m_c - m) * l_c for m_c, l_c, _ in parts)
    acc = sum(jnp.exp2(m_c - m) * a_c for m_c, _, a_c in parts)
    o = acc / l
    for c in range(n_half):
        o_ref[c] = o[:, c * LANES:(c + 1) * LANES]

    @pl.when(b == pl.num_programs(0) - 1)
    def _():
        wait_slot(1 - slot)


def _sample_attn(page_table, q_lat, q_rope, c_new, kr_new, cache_ckv, cache_krope_t):
    nreq, n_pages = page_table.shape
    _, page, kv_lora = cache_ckv.shape
    n_half = q_lat.shape[0]
    nq = q_lat.shape[1] // nreq
    lpad = c_new.shape[1]
    grid_spec = pltpu.PrefetchScalarGridSpec(
        num_scalar_prefetch=1, grid=(nreq,),
        in_specs=[
            pl.BlockSpec((n_half, nq, LANES), lambda b, pt: (0, b, 0)),
            pl.BlockSpec((nq, LANES), lambda b, pt: (b, 0)),
            pl.BlockSpec((1, lpad, kv_lora), lambda b, pt: (b, 0, 0)),
            pl.BlockSpec((1, lpad, QK_ROPE), lambda b, pt: (b, 0, 0)),
            pl.BlockSpec(memory_space=pl.ANY),
            pl.BlockSpec(memory_space=pl.ANY),
        ],
        out_specs=pl.BlockSpec((n_half, nq, LANES), lambda b, pt: (0, b, 0)),
        scratch_shapes=[
            pltpu.VMEM((2, n_pages, page, kv_lora), f32),
            pltpu.VMEM((2, n_pages, QK_ROPE, page), f32),
            pltpu.VMEM((n_pages * page, kv_lora), bf16),
            pltpu.VMEM((QK_ROPE, n_pages * page), bf16),
            pltpu.SemaphoreType.DMA((2,)),
            pltpu.SemaphoreType.DMA((2,)),
        ],
    )
    return pl.pallas_call(
        functools.partial(_sample_attn_kernel, n_pages=n_pages, page=page,
                          n_chunks=min(SAMPLE_ATTN_CHUNKS, n_pages)),
        grid_spec=grid_spec,
        out_shape=jax.ShapeDtypeStruct((n_half, nreq * nq, LANES), f32),
        compiler_params=_params("arbitrary"),
        name="sample_attn",
    )(page_table.reshape(-1), q_lat, q_rope, c_new, kr_new, cache_ckv, cache_krope_t)


def _mla_out_kernel(*refs, absorbed, rows):
    if absorbed:
        ol_ref, w_uv_ref, z_ref, x_ref, w_out_ref, post_g_ref, xo_ref = refs
        def head_rows(hd):
            halves = [ol_ref[c, pl.ds(hd, rows, stride=N_HEADS), :] for c in range(ol_ref.shape[0])]
            return jnp.concatenate(halves, axis=-1).astype(bf16)

        parts = []
        for g in range(N_HEADS // 2):
            parts.append(_dot(head_rows(2 * g), w_uv_ref[2 * g]) + _dot(head_rows(2 * g + 1), w_uv_ref[2 * g + 1]))
        o = jnp.concatenate(parts, axis=-1)
    else:
        o_ref, z_ref, x_ref, w_out_ref, post_g_ref, xo_ref = refs
        o = o_ref[...]
    g = (o * jax.nn.silu(z_ref[...])).astype(bf16)
    m = _dot(g, w_out_ref[...])
    xo_ref[...] = x_ref[...] + _rms(m, post_g_ref[...])


def _mla_out(o, z, x, w_out, post_g, w_uv_pad=None, *, tile):
    rows, d = x.shape
    z_dim = z.shape[1]
    nt = rows // tile
    absorbed = w_uv_pad is not None
    row_spec = lambda w: pl.BlockSpec((tile, w), lambda i: (i, 0))
    if absorbed:
        assert nt == 1
        in_specs = [_const_spec(o.shape), _const_spec(w_uv_pad.shape)]
        args = [o, w_uv_pad]
    else:
        in_specs = [row_spec(z_dim)]
        args = [o]
    in_specs += [row_spec(z_dim), row_spec(d), _const_spec(w_out.shape), _const_spec((1, d))]
    args += [z, x, w_out, post_g]
    return pl.pallas_call(
        functools.partial(_mla_out_kernel, absorbed=absorbed, rows=rows),
        grid=(nt,), in_specs=in_specs, out_specs=row_spec(d),
        out_shape=jax.ShapeDtypeStruct((rows, d), f32),
        compiler_params=_params("arbitrary"),
        name="mla_out_absorbed" if absorbed else "mla_out",
    )(*args)


def _rope_tables(pos, scale):
    inv = ROPE_THETA ** (-jnp.arange(HALF_ROPE, dtype=f32) / HALF_ROPE)
    ang = pos.astype(f32)[:, None] * inv[None, :]
    cos, sin = jnp.cos(ang), jnp.sin(ang)
    r = pos.shape[0]
    zeros = lambda w: jnp.zeros((r, w), f32)
    c = jnp.concatenate([cos, cos, zeros(LANES - QK_ROPE - QK_NOPE), jnp.ones((r, QK_NOPE), f32)], axis=-1)
    s1 = jnp.concatenate([-sin, zeros(LANES - HALF_ROPE)], axis=-1)
    s2 = jnp.concatenate([zeros(HALF_ROPE), sin, zeros(LANES - QK_ROPE)], axis=-1)
    return c * scale, s1 * scale, s2 * scale


def _head_blocks(w_rope, w_nope):
    pad = jnp.zeros(w_nope.shape[:-1] + (LANES - QK_ROPE - QK_NOPE,), w_nope.dtype)
    blk = jnp.concatenate([w_rope, pad, w_nope], axis=-1)
    return blk.reshape(blk.shape[:-2] + (N_HEADS * LANES,))


def kernel(x_prompt, x_sample, cache_ckv, cache_krope, state_conv, page_table, meta_tokens,
           pre_norm_g, post_norm_g, w_in_conv, conv_w, w_out_conv, kv_norm_g, w_dkv,
           kv_lat_norm_g, w_uk, w_uv, w_in_mla, q_norm_g, w_uq, w_out_mla):
    bp, seq, d = x_prompt.shape
    bs, ls, _ = x_sample.shape
    n_a = w_in_conv.shape[0]
    n_b = w_in_mla.shape[0]
    c_dim = conv_w.shape[2]
    kv_lora = kv_lat_norm_g.shape[0]
    q_lora = q_norm_g.shape[1]
    past_len = page_table.shape[1] * cache_ckv.shape[1]

    row = lambda v: v.reshape(1, -1).astype(f32)
    w_in_conv_b = w_in_conv.astype(bf16)
    w_out_conv_b = w_out_conv.astype(bf16)
    w_in_mla_b = w_in_mla.astype(bf16)
    w_out_mla_b = w_out_mla.astype(bf16)
    w_dkv_pad = jnp.concatenate([w_dkv, jnp.zeros((d, LANES - QK_ROPE), w_dkv.dtype)], axis=-1).astype(bf16)
    uq = w_uq.reshape(n_b, q_lora, N_HEADS, QK_NOPE + QK_ROPE)
    w_uq_pad = _head_blocks(uq[..., QK_NOPE:], uq[..., :QK_NOPE]).astype(bf16)
    w_uk_blk = _head_blocks(jnp.zeros((kv_lora, N_HEADS, QK_ROPE), w_uk.dtype), w_uk)
    w_uk_pad = w_uk_blk.astype(bf16)
    w_ukt_pad = jnp.transpose(w_uk_blk.reshape(kv_lora, N_HEADS, LANES), (1, 2, 0)).astype(bf16)
    w_uv_blk = jnp.concatenate([w_uv, jnp.zeros((kv_lora, N_HEADS, LANES - V_HEAD), w_uv.dtype)], axis=-1)
    w_uv_blk = w_uv_blk.reshape(kv_lora, N_HEADS * LANES).astype(bf16)
    uv = jnp.transpose(w_uv, (1, 0, 2))
    zv = jnp.zeros_like(uv)
    even = (jnp.arange(N_HEADS) % 2 == 0)[:, None, None]
    w_uv_pad = jnp.concatenate([jnp.where(even, uv, zv), jnp.where(even, zv, uv)], axis=-1).astype(bf16)

    def trunk_a(x, inits, *, nseq, tile, shift):
        states = []
        for l in range(n_a):
            x, st = _conv_layer(x, inits[l], row(pre_norm_g[l]), row(post_norm_g[l]), w_in_conv_b[l],
                                conv_w[l], w_out_conv_b[l], nseq=nseq, tile=tile, shift=shift)
            states.append(st)
        return x, states

    zero_init = jnp.zeros((1, SUBLANES, c_dim), f32)
    xm, meta_states = trunk_a(meta_tokens.astype(f32), [zero_init] * n_a, nseq=1, tile=N_META, shift=1)
    tabs_meta = _rope_tables(jnp.arange(N_META, dtype=jnp.int32), 1.0)
    c_meta, kr_meta, k_meta, v_meta = _latent(xm, row(kv_norm_g), w_dkv_pad, row(kv_lat_norm_g), tabs_meta,
                                              w_uk_pad, w_uv_blk, nseq=1, tile=N_META)

    tile_p = min(ROW_TILE, seq)
    inits_p = [jnp.concatenate([jnp.zeros((1, SUBLANES - 2, c_dim), f32), st], axis=1) for st in meta_states]
    xp, prompt_states = trunk_a(x_prompt.reshape(bp * seq, d), inits_p, nseq=bp, tile=tile_p, shift=1)
    pos_p = N_META + jnp.arange(seq, dtype=jnp.int32)
    c_p, kr_p, k_p, v_p = _latent(xp, row(kv_norm_g), w_dkv_pad, row(kv_lat_norm_g), _rope_tables(pos_p, 1.0),
                                  w_uk_pad, w_uv_blk, nseq=bp, tile=tile_p)
    tabs_q = _rope_tables(pos_p, SOFTMAX_SCALE * LOG2_E)
    attn_tile = min(ATTN_TILE, seq)
    for j in range(n_b):
        l = n_a + j
        q, z = _mla_query(xp, row(pre_norm_g[l]), w_in_mla_b[j], row(q_norm_g[j]), w_uq_pad[j], tabs_q,
                          nseq=bp, tile=tile_p)
        o = _prompt_attn(q.reshape(bp, seq, -1), k_p.reshape(bp, seq, -1), v_p.reshape(bp, seq, -1),
                         k_meta, v_meta, tile=attn_tile)
        xp = _mla_out(o.reshape(bp * seq, -1), z, xp, w_out_mla_b[j], row(post_norm_g[l]), tile=tile_p)
    y_prompt = xp.reshape(bp, seq, d)
    bcast = lambda a: jnp.broadcast_to(a[None], (bp,) + a.shape)
    ckv_prompt = jnp.concatenate([bcast(c_meta), c_p.reshape(bp, seq, kv_lora)], axis=1)
    krope_prompt = jnp.concatenate([bcast(kr_meta), kr_p.reshape(bp, seq, QK_ROPE)], axis=1)
    conv_prompt = jnp.stack([st for st in prompt_states])

    rs = bs * ls
    xs = jnp.transpose(x_sample, (1, 0, 2)).reshape(rs, d)
    inits_s = [jnp.transpose(state_conv[l], (1, 0, 2)).reshape(1, (CONV_WIDTH - 1) * bs, c_dim) for l in range(n_a)]
    xs, sample_states = trunk_a(xs, inits_s, nseq=1, tile=rs, shift=bs)
    conv_sample = jnp.stack([jnp.transpose(st.reshape(CONV_WIDTH - 1, bs, c_dim), (1, 0, 2)) for st in sample_states])
    xs = jnp.transpose(xs.reshape(ls, bs, d), (1, 0, 2)).reshape(rs, d)
    pos_s = jnp.tile(past_len + jnp.arange(ls, dtype=jnp.int32), bs)
    c_s, kr_s = _latent(xs, row(kv_norm_g), w_dkv_pad, row(kv_lat_norm_g), _rope_tables(pos_s, 1.0),
                        nseq=1, tile=rs)
    lpad = 16
    c_new = jnp.pad(c_s.reshape(bs, ls, kv_lora), ((0, 0), (0, lpad - ls), (0, 0)))
    kr_new = jnp.pad(kr_s.reshape(bs, ls, QK_ROPE), ((0, 0), (0, lpad - ls), (0, 0)))
    tabs_qs = _rope_tables(pos_s, SOFTMAX_SCALE * LOG2_E)
    cache_krope_t = jnp.swapaxes(cache_krope, 1, 2)
    for j in range(n_b):
        l = n_a + j
        q_lat, q_rope, z = _mla_query(xs, row(pre_norm_g[l]), w_in_mla_b[j], row(q_norm_g[j]), w_uq_pad[j],
                                      tabs_qs, w_ukt_pad, nseq=1, tile=rs)
        o_lat = _sample_attn(page_table, q_lat, q_rope, c_new, kr_new, cache_ckv, cache_krope_t)
        xs = _mla_out(o_lat, z, xs, w_out_mla_b[j], row(post_norm_g[l]), w_uv_pad, tile=rs)
    y_sample = xs.reshape(bs, ls, d)
    ckv_sample = c_s.reshape(bs, ls, kv_lora)
    krope_sample = kr_s.reshape(bs, ls, QK_ROPE)

    return (y_prompt, y_sample, ckv_prompt, krope_prompt, conv_prompt, ckv_sample, krope_sample, conv_sample)
```

```python
import functools

import jax
import jax.numpy as jnp
from jax import lax
from jax.experimental import pallas as pl
from jax.experimental.pallas import tpu as pltpu

N_META = 16
N_HEADS = 16
QK_NOPE = 64
QK_ROPE = 32
V_HEAD = 64
ROPE_THETA = 10000.0
RMS_EPS = 1e-6
CONV_WIDTH = 3
SOFTMAX_SCALE = (QK_NOPE + QK_ROPE) ** -0.5
LOG2_E = 1.4426950408889634

LANES = 128
SUBLANES = 8
HALF_ROPE = QK_ROPE // 2
VMEM_LIMIT = 56 * 1024 * 1024

ROW_TILE = 512
ATTN_TILE = 1024
ATTN_HEADS_PER_STEP = 4
SAMPLE_ATTN_CHUNKS = 4

bf16 = jnp.bfloat16
f32 = jnp.float32


def _rms(x, g):
    return x * lax.rsqrt(jnp.mean(x * x, axis=-1, keepdims=True) + RMS_EPS) * g


def _dot(a, b):
    return jnp.dot(a, b, preferred_element_type=f32)


def _dot_nt(a, b):
    return lax.dot_general(a, b, (((1,), (1,)), ((), ())), preferred_element_type=f32)


def _rope_block(blk, cos, s1, s2):
    return (blk * cos + pltpu.roll(blk, LANES - HALF_ROPE, 1) * s1 + pltpu.roll(blk, HALF_ROPE, 1) * s2)


def _params(*sem):
    return pltpu.CompilerParams(dimension_semantics=sem, vmem_limit_bytes=VMEM_LIMIT)


def _const_spec(shape):
    nd = len(shape)
    return pl.BlockSpec(shape, lambda *_: (0,) * nd)


def _conv_layer_kernel(x_ref, init_ref, pre_g_ref, post_g_ref, w_in_ref, cw_ref, w_out_ref,
                       xo_ref, st_ref, vbuf, *, tile, off, shift, c_dim):
    i = pl.program_id(1)

    @pl.when(i == 0)
    def _():
        vbuf[0:off, :] = init_ref[0]

    x = x_ref[...]
    h = _rms(x, pre_g_ref[...]).astype(bf16)

    def proj(k):
        return _dot(h, w_in_ref[:, k * c_dim:(k + 1) * c_dim])

    vbuf[off:off + tile, :] = proj(1) * proj(2)
    cw = cw_ref[...]
    y = cw[0:1] * vbuf[off - 2 * shift:off - 2 * shift + tile, :]
    y = y + cw[1:2] * vbuf[off - shift:off - shift + tile, :]
    y = y + cw[2:3] * vbuf[off:off + tile, :]
    z = proj(3)
    g = (proj(0) * y * jax.nn.silu(z)).astype(bf16)
    m = _dot(g, w_out_ref[...])
    xo_ref[...] = x + _rms(m, post_g_ref[...])

    @pl.when(i == pl.num_programs(1) - 1)
    def _():
        st_ref[0] = vbuf[off + tile - 2 * shift:off + tile, :]

    vbuf[0:off, :] = vbuf[tile:tile + off, :]


def _conv_layer(x, init, pre_g, post_g, w_in, cw, w_out, *, nseq, tile, shift):
    rows, d = x.shape
    c_dim = cw.shape[1]
    t = rows // nseq
    nt = t // tile
    off = init.shape[1]
    ninit = init.shape[0]
    kern = functools.partial(_conv_layer_kernel, tile=tile, off=off, shift=shift, c_dim=c_dim)
    return pl.pallas_call(
        kern,
        grid=(nseq, nt),
        in_specs=[
            pl.BlockSpec((tile, d), lambda b, i: (b * nt + i, 0)),
            pl.BlockSpec((1, off, c_dim), (lambda b, i: (b, 0, 0)) if ninit > 1 else (lambda b, i: (0, 0, 0))),
            _const_spec((1, d)), _const_spec((1, d)),
            _const_spec(w_in.shape), _const_spec(cw.shape), _const_spec(w_out.shape),
        ],
        out_specs=[
            pl.BlockSpec((tile, d), lambda b, i: (b * nt + i, 0)),
            pl.BlockSpec((1, 2 * shift, c_dim), lambda b, i: (b, 0, 0)),
        ],
        out_shape=[jax.ShapeDtypeStruct((rows, d), f32),
                   jax.ShapeDtypeStruct((nseq, 2 * shift, c_dim), f32)],
        scratch_shapes=[pltpu.VMEM((off + tile, c_dim), f32)],
        compiler_params=_params("arbitrary", "arbitrary"),
        name="conv_layer",
    )(x, init, pre_g, post_g, w_in, cw, w_out)


def _latent_kernel(*refs, kv_lora, with_kv):
    if with_kv:
        (x_ref, g_ref, w_dkv_ref, lat_g_ref, cos_ref, s1_ref, s2_ref, w_uk_ref, w_uv_ref,
         c_ref, kr_ref, k_ref, v_ref) = refs
    else:
        x_ref, g_ref, w_dkv_ref, lat_g_ref, cos_ref, s1_ref, s2_ref, c_ref, kr_ref = refs
    xn = _rms(x_ref[...], g_ref[...]).astype(bf16)
    ckr = _dot(xn, w_dkv_ref[...])
    c = _rms(ckr[:, :kv_lora], lat_g_ref[...])
    krb = _rope_block(ckr[:, kv_lora:kv_lora + LANES], cos_ref[...], s1_ref[...], s2_ref[...])
    c_ref[...] = c
    kr_ref[...] = krb[:, :QK_ROPE]
    if with_kv:
        cb = c.astype(bf16)
        kn = _dot(cb, w_uk_ref[...])
        for h in range(N_HEADS):
            k_ref[:, h * LANES:(h + 1) * LANES] = (kn[:, h * LANES:(h + 1) * LANES] + krb).astype(bf16)
        pos = lax.broadcasted_iota(jnp.int32, (1, N_HEADS * LANES), 1) % (2 * LANES)
        ones_lanes = jnp.where(pos < V_HEAD, 0.0, jnp.where(pos < V_HEAD + LANES, 1.0, 0.0)).astype(f32)
        v_ref[...] = (_dot(cb, w_uv_ref[...]) + ones_lanes).astype(bf16)


def _latent(x, g, w_dkv_pad, lat_g, tabs, w_uk_pad=None, w_uv=None, *, t_out, tile):
    nseq, t_in, d = x.shape
    kv_lora = lat_g.shape[1]
    nt = t_in // tile
    with_kv = w_uk_pad is not None
    row_spec = lambda w: pl.BlockSpec((None, tile, w), lambda b, i: (b, i, 0))
    tab_spec = pl.BlockSpec((tile, LANES), lambda b, i: (i, 0))
    in_specs = [row_spec(d), _const_spec((1, d)), _const_spec(w_dkv_pad.shape), _const_spec((1, kv_lora)),
                tab_spec, tab_spec, tab_spec]
    args = [x, g, w_dkv_pad, lat_g, *tabs]
    out_specs = [row_spec(kv_lora), row_spec(QK_ROPE)]
    out_shape = [jax.ShapeDtypeStruct((nseq, t_out, kv_lora), f32), jax.ShapeDtypeStruct((nseq, t_out, QK_ROPE), f32)]
    if with_kv:
        in_specs += [_const_spec(w_uk_pad.shape), _const_spec(w_uv.shape)]
        args += [w_uk_pad, w_uv]
        out_specs += [row_spec(N_HEADS * LANES), row_spec(N_HEADS * LANES)]
        out_shape += [jax.ShapeDtypeStruct((nseq, t_out, N_HEADS * LANES), bf16),
                      jax.ShapeDtypeStruct((nseq, t_out, N_HEADS * LANES), bf16)]
    return pl.pallas_call(
        functools.partial(_latent_kernel, kv_lora=kv_lora, with_kv=with_kv),
        grid=(nseq, nt), in_specs=in_specs, out_specs=out_specs, out_shape=out_shape,
        compiler_params=_params("arbitrary", "arbitrary"),
        name="latent_kv" if with_kv else "latent",
    )(*args)


def _mla_query_kernel(*refs, q_lora, absorbed, rows):
    if absorbed:
        (x_ref, pre_g_ref, w_in_ref, qg_ref, w_uq_ref, cos_ref, s1_ref, s2_ref, w_ukt_ref,
         ql_ref, qr_ref, z_ref) = refs
    else:
        x_ref, pre_g_ref, w_in_ref, qg_ref, w_uq_ref, cos_ref, s1_ref, s2_ref, q_ref, z_ref = refs
    h = _rms(x_ref[...], pre_g_ref[...]).astype(bf16)
    q_lat = _dot(h, w_in_ref[:, :q_lora])
    z_ref[...] = _dot(h, w_in_ref[:, q_lora:]).astype(z_ref.dtype)
    qn = _rms(q_lat, qg_ref[...]).astype(bf16)
    q = _dot(qn, w_uq_ref[...])
    cos, s1, s2 = cos_ref[...], s1_ref[...], s2_ref[...]
    for hd in range(N_HEADS):
        blk = _rope_block(q[:, hd * LANES:(hd + 1) * LANES], cos, s1, s2)
        if absorbed:
            ql = _dot(blk.astype(bf16), w_ukt_ref[hd])
            for c in range(ql.shape[1] // LANES):
                ql_ref[c, pl.ds(hd, rows, stride=N_HEADS), :] = ql[:, c * LANES:(c + 1) * LANES]
            qr_ref[pl.ds(hd, rows, stride=N_HEADS), :] = blk
        else:
            q_ref[:, hd * LANES:(hd + 1) * LANES] = blk.astype(bf16)


def _mla_query(x, pre_g, w_in, qg, w_uq_pad, tabs, w_ukt_pad=None, *, nseq, tile):
    rows, d = x.shape
    q_lora = qg.shape[1]
    z_dim = w_in.shape[1] - q_lora
    nt = rows // nseq // tile
    absorbed = w_ukt_pad is not None
    row_spec = lambda w: pl.BlockSpec((tile, w), lambda b, i: (b * nt + i, 0))
    tab_spec = pl.BlockSpec((tile, LANES), lambda b, i: (i, 0))
    in_specs = [row_spec(d), _const_spec((1, d)), _const_spec(w_in.shape), _const_spec((1, q_lora)),
                _const_spec(w_uq_pad.shape), tab_spec, tab_spec, tab_spec]
    args = [x, pre_g, w_in, qg, w_uq_pad, *tabs]
    if absorbed:
        assert nseq == 1 and nt == 1
        kv_lora = w_ukt_pad.shape[2]
        in_specs.append(_const_spec(w_ukt_pad.shape))
        args.append(w_ukt_pad)
        ql_shape = (kv_lora // LANES, rows * N_HEADS, LANES)
        out_specs = [_const_spec(ql_shape), _const_spec((rows * N_HEADS, LANES)), row_spec(z_dim)]
        out_shape = [jax.ShapeDtypeStruct(ql_shape, f32),
                     jax.ShapeDtypeStruct((rows * N_HEADS, LANES), f32),
                     jax.ShapeDtypeStruct((rows, z_dim), bf16)]
    else:
        out_specs = [row_spec(N_HEADS * LANES), row_spec(z_dim)]
        out_shape = [jax.ShapeDtypeStruct((rows, N_HEADS * LANES), bf16),
                     jax.ShapeDtypeStruct((rows, z_dim), bf16)]
    return pl.pallas_call(
        functools.partial(_mla_query_kernel, q_lora=q_lora, absorbed=absorbed, rows=rows),
        grid=(nseq, nt), in_specs=in_specs, out_specs=out_specs, out_shape=out_shape,
        compiler_params=_params("arbitrary", "arbitrary"),
        name="mla_query_absorbed" if absorbed else "mla_query",
    )(*args)


def _prompt_attn_kernel(q_ref, k_ref, v_ref, o_ref, *, tile):
    i = pl.program_id(2)
    neg = jnp.finfo(f32).min
    heads = range(q_ref.shape[2] // LANES)
    n_sub = 2
    sub = tile // n_sub

    def keys(hh, start, size):
        return k_ref[0, pl.ds(start, size), hh * LANES:(hh + 1) * LANES]

    def values(hh, start, size):
        return v_ref[0, pl.ds(start, size), hh * LANES:(hh + 1) * LANES]

    d0 = pl.multiple_of(i * tile, tile)
    head_rows = sub - N_META

    def init_rows(hh, r):
        q = q_ref[0, r * sub:(r + 1) * sub, hh * LANES:(hh + 1) * LANES]
        nk = (r + 1) * sub
        row = lax.broadcasted_iota(jnp.int32, (sub, nk), 0) + (r * sub + N_META)
        col = lax.broadcasted_iota(jnp.int32, (sub, nk), 1)
        s = jnp.where(col <= row, _dot_nt(q, keys(hh, d0, nk)), neg)
        m = jnp.max(s, axis=-1, keepdims=True)
        acc = _dot(jnp.exp2(s - m).astype(bf16), values(hh, d0, nk))
        sc = _dot_nt(q[head_rows:], keys(hh, d0 + nk, N_META))
        rowc = lax.broadcasted_iota(jnp.int32, (N_META, N_META), 0)
        colc = lax.broadcasted_iota(jnp.int32, (N_META, N_META), 1)
        sc = jnp.where(colc <= rowc, sc, neg)
        m_c = jnp.max(sc, axis=-1, keepdims=True)
        acc_c = _dot(jnp.exp2(sc - m_c).astype(bf16), values(hh, d0 + nk, N_META))
        m_tail = jnp.maximum(m[head_rows:], m_c)
        acc_tail = jnp.exp2(m[head_rows:] - m_tail) * acc[head_rows:] + jnp.exp2(m_c - m_tail) * acc_c
        m_b = jnp.concatenate([jnp.broadcast_to(m[:head_rows], (head_rows, LANES)),
                               jnp.broadcast_to(m_tail, (N_META, LANES))], axis=0)
        return m_b, jnp.concatenate([acc[:head_rows], acc_tail], axis=0)

    carry = []
    for hh in heads:
        parts = [init_rows(hh, r) for r in range(n_sub)]
        m_rows = jnp.concatenate([pt[0] for pt in parts], axis=0)
        carry += [jnp.max(m_rows, axis=-1, keepdims=True), jnp.concatenate([pt[1] for pt in parts], axis=0)]

    qs = [q_ref[0, :, hh * LANES:(hh + 1) * LANES] for hh in heads]

    def step(carry, start, size):
        out = []
        for hh in heads:
            m, acc = carry[2 * hh], carry[2 * hh + 1]
            s = _dot_nt(qs[hh], keys(hh, start, size))
            m_new = jnp.maximum(m, jnp.max(s, axis=-1, keepdims=True))
            alpha = jnp.exp2(m - m_new)
            p = jnp.exp2(s - m_new)
            out += [m_new, alpha * acc + _dot(p.astype(bf16), values(hh, start, size))]
        return tuple(out)

    carry = lax.fori_loop(0, i // 2, lambda j, c: step(c, pl.multiple_of(j * 2 * tile, 2 * tile), 2 * tile),
                          tuple(carry))
    carry = lax.cond(i % 2 == 1, lambda c: step(c, pl.multiple_of((i - 1) * tile, tile), tile), lambda c: c, carry)
    lane = lax.broadcasted_iota(jnp.int32, (tile, LANES), 1)
    for g in range(len(heads) // 2):
        a_even, a_odd = carry[4 * g + 1], carry[4 * g + 3]
        num = jnp.where(lane < V_HEAD, a_even, a_odd)
        den = pltpu.roll(jnp.where(lane < V_HEAD, a_odd, a_even), V_HEAD, 1)
        o_ref[0, :, g * LANES:(g + 1) * LANES] = (num / den).astype(o_ref.dtype)


def _prompt_attn(q, k, v, *, tile):
    b, t, _ = q.shape
    tk = k.shape[1]
    nq = t // tile
    hps = ATTN_HEADS_PER_STEP
    return pl.pallas_call(
        functools.partial(_prompt_attn_kernel, tile=tile),
        grid=(b, N_HEADS // hps, nq),
        in_specs=[
            pl.BlockSpec((1, tile, hps * LANES), lambda b, g, i: (b, i, g)),
            pl.BlockSpec((1, tk, hps * LANES), lambda b, g, i: (b, 0, g)),
            pl.BlockSpec((1, tk, hps * LANES), lambda b, g, i: (b, 0, g)),
        ],
        out_specs=pl.BlockSpec((1, tile, hps * V_HEAD), lambda b, g, i: (b, i, g)),
        out_shape=jax.ShapeDtypeStruct((b, t, N_HEADS * V_HEAD), bf16),
        compiler_params=_params("arbitrary", "arbitrary", "arbitrary"),
        name="prompt_attn",
    )(q, k, v)


def _sample_attn_kernel(pt_ref, ql_ref, qr_ref, cn_ref, krn_ref, ckv_hbm, krt_hbm, o_ref,
                        kv_land, kr_land, kb16, kr16, sem_kv, sem_kr, *, n_pages, page, n_chunks):
    b = pl.program_id(0)
    slot = lax.rem(b, 2)

    def page_copies(req, slot_, p):
        pg = pt_ref[req * n_pages + p]
        return (pltpu.make_async_copy(ckv_hbm.at[pg], kv_land.at[slot_, p], sem_kv.at[slot_]),
                pltpu.make_async_copy(krt_hbm.at[pg], kr_land.at[slot_, p], sem_kr.at[slot_]))

    def wait_slot(slot_):
        pltpu.make_async_copy(ckv_hbm.at[pl.ds(0, n_pages)], kv_land.at[slot_], sem_kv.at[slot_]).wait()
        pltpu.make_async_copy(krt_hbm.at[pl.ds(0, n_pages)], kr_land.at[slot_], sem_kr.at[slot_]).wait()

    @pl.when(b == 0)
    def _():
        def body(p, c):
            for cp in page_copies(0, 0, p):
                cp.start()
            return c
        lax.fori_loop(0, n_pages, body, 0)

    wait_slot(slot)
    nxt = jnp.minimum(b + 1, pl.num_programs(0) - 1)
    for p in range(n_pages):
        for cp in page_copies(nxt, 1 - slot, p):
            cp.start()

    n_half = ql_ref.shape[0]
    ql = jnp.concatenate([ql_ref[c] for c in range(n_half)], axis=-1).astype(bf16)
    qr = qr_ref[:, :QK_ROPE].astype(bf16)
    nq = ql.shape[0]

    ppc = n_pages // n_chunks
    parts = []
    for c in range(n_chunks):
        for p in range(c * ppc, (c + 1) * ppc):
            kb16[p * page:(p + 1) * page, :] = kv_land[slot, p].astype(bf16)
            kr16[:, p * page:(p + 1) * page] = kr_land[slot, p].astype(bf16)
        kb = kb16[c * ppc * page:(c + 1) * ppc * page, :]
        half = ppc * page // 2
        s = jnp.concatenate([_dot_nt(ql, kb[:half]), _dot_nt(ql, kb[half:])], axis=-1)
        s = s + _dot(qr, kr16[:, c * ppc * page:(c + 1) * ppc * page])
        m_c = jnp.max(s, axis=-1, keepdims=True)
        p_c = jnp.exp2(s - m_c)
        pb = p_c.astype(bf16)
        a_c = _dot(pb[:, :half], kb[:half]) + _dot(pb[:, half:], kb[half:])
        parts.append((m_c, jnp.sum(p_c, axis=-1, keepdims=True), a_c))
    cn = cn_ref[0].astype(bf16)
    krn = krn_ref[0].astype(bf16)
    npad = cn.shape[0]
    s_new = _dot_nt(ql, cn) + _dot_nt(qr, krn)
    row = lax.broadcasted_iota(jnp.int32, (nq, npad), 0)
    col = lax.broadcasted_iota(jnp.int32, (nq, npad), 1)
    s_new = jnp.where(col * N_HEADS <= row, s_new, jnp.finfo(f32).min)
    m_n = jnp.max(s_new, axis=-1, keepdims=True)
    p_n = jnp.exp2(s_new - m_n)
    parts.append((m_n, jnp.sum(p_n, axis=-1, keepdims=True), _dot(p_n.astype(bf16), cn)))

    m = functools.reduce(jnp.maximum, [pt_[0] for pt_ in parts])
    l = sum(jnp.exp2(m_c - m) * l_c for m_c, l_c, _ in parts)
    acc = sum(jnp.exp2(m_c - m) * a_c for m_c, _, a_c in parts)
    o = acc / l
    for c in range(n_half):
        o_ref[c] = o[:, c * LANES:(c + 1) * LANES]

    @pl.when(b == pl.num_programs(0) - 1)
    def _():
        wait_slot(1 - slot)


def _sample_attn(page_table, q_lat, q_rope, c_new, kr_new, cache_ckv, cache_krope_t):
    nreq, n_pages = page_table.shape
    _, page, kv_lora = cache_ckv.shape
    n_half = q_lat.shape[0]
    nq = q_lat.shape[1] // nreq
    lpad = c_new.shape[1]
    grid_spec = pltpu.PrefetchScalarGridSpec(
        num_scalar_prefetch=1, grid=(nreq,),
        in_specs=[
            pl.BlockSpec((n_half, nq, LANES), lambda b, pt: (0, b, 0)),
            pl.BlockSpec((nq, LANES), lambda b, pt: (b, 0)),
            pl.BlockSpec((1, lpad, kv_lora), lambda b, pt: (b, 0, 0)),
            pl.BlockSpec((1, lpad, QK_ROPE), lambda b, pt: (b, 0, 0)),
            pl.BlockSpec(memory_space=pl.ANY),
            pl.BlockSpec(memory_space=pl.ANY),
        ],
        out_specs=pl.BlockSpec((n_half, nq, LANES), lambda b, pt: (0, b, 0)),
        scratch_shapes=[
            pltpu.VMEM((2, n_pages, page, kv_lora), f32),
            pltpu.VMEM((2, n_pages, QK_ROPE, page), f32),
            pltpu.VMEM((n_pages * page, kv_lora), bf16),
            pltpu.VMEM((QK_ROPE, n_pages * page), bf16),
            pltpu.SemaphoreType.DMA((2,)),
            pltpu.SemaphoreType.DMA((2,)),
        ],
    )
    return pl.pallas_call(
        functools.partial(_sample_attn_kernel, n_pages=n_pages, page=page,
                          n_chunks=min(SAMPLE_ATTN_CHUNKS, n_pages)),
        grid_spec=grid_spec,
        out_shape=jax.ShapeDtypeStruct((n_half, nreq * nq, LANES), f32),
        compiler_params=_params("arbitrary"),
        name="sample_attn",
    )(page_table.reshape(-1), q_lat, q_rope, c_new, kr_new, cache_ckv, cache_krope_t)


def _mla_out_kernel(*refs, absorbed, rows):
    if absorbed:
        ol_ref, w_uv_ref, z_ref, x_ref, w_out_ref, post_g_ref, xo_ref = refs
        def head_rows(hd):
            halves = [ol_ref[c, pl.ds(hd, rows, stride=N_HEADS), :] for c in range(ol_ref.shape[0])]
            return jnp.concatenate(halves, axis=-1).astype(bf16)

        parts = []
        for g in range(N_HEADS // 2):
            parts.append(_dot(head_rows(2 * g), w_uv_ref[2 * g]) + _dot(head_rows(2 * g + 1), w_uv_ref[2 * g + 1]))
        o = jnp.concatenate(parts, axis=-1)
    else:
        o_ref, z_ref, x_ref, w_out_ref, post_g_ref, xo_ref = refs
        o = o_ref[...]
    g = (o.astype(f32) * jax.nn.silu(z_ref[...].astype(f32))).astype(bf16)
    m = _dot(g, w_out_ref[...])
    xo_ref[...] = x_ref[...] + _rms(m, post_g_ref[...])


def _mla_out(o, z, x, w_out, post_g, w_uv_pad=None, *, tile):
    rows, d = x.shape
    z_dim = z.shape[1]
    nt = rows // tile
    absorbed = w_uv_pad is not None
    row_spec = lambda w: pl.BlockSpec((tile, w), lambda i: (i, 0))
    if absorbed:
        assert nt == 1
        in_specs = [_const_spec(o.shape), _const_spec(w_uv_pad.shape)]
        args = [o, w_uv_pad]
    else:
        in_specs = [row_spec(z_dim)]
        args = [o]
    in_specs += [row_spec(z_dim), row_spec(d), _const_spec(w_out.shape), _const_spec((1, d))]
    args += [z, x, w_out, post_g]
    return pl.pallas_call(
        functools.partial(_mla_out_kernel, absorbed=absorbed, rows=rows),
        grid=(nt,), in_specs=in_specs, out_specs=row_spec(d),
        out_shape=jax.ShapeDtypeStruct((rows, d), f32),
        compiler_params=_params("arbitrary"),
        name="mla_out_absorbed" if absorbed else "mla_out",
    )(*args)


def _rope_tables(pos, scale):
    inv = ROPE_THETA ** (-jnp.arange(HALF_ROPE, dtype=f32) / HALF_ROPE)
    ang = pos.astype(f32)[:, None] * inv[None, :]
    cos, sin = jnp.cos(ang), jnp.sin(ang)
    r = pos.shape[0]
    zeros = lambda w: jnp.zeros((r, w), f32)
    c = jnp.concatenate([cos, cos, zeros(LANES - QK_ROPE - QK_NOPE), jnp.ones((r, QK_NOPE), f32)], axis=-1)
    s1 = jnp.concatenate([-sin, zeros(LANES - HALF_ROPE)], axis=-1)
    s2 = jnp.concatenate([zeros(HALF_ROPE), sin, zeros(LANES - QK_ROPE)], axis=-1)
    return c * scale, s1 * scale, s2 * scale


def _head_blocks(w_rope, w_nope):
    pad = jnp.zeros(w_nope.shape[:-1] + (LANES - QK_ROPE - QK_NOPE,), w_nope.dtype)
    blk = jnp.concatenate([w_rope, pad, w_nope], axis=-1)
    return blk.reshape(blk.shape[:-2] + (N_HEADS * LANES,))


def kernel(x_prompt, x_sample, cache_ckv, cache_krope, state_conv, page_table, meta_tokens,
           pre_norm_g, post_norm_g, w_in_conv, conv_w, w_out_conv, kv_norm_g, w_dkv,
           kv_lat_norm_g, w_uk, w_uv, w_in_mla, q_norm_g, w_uq, w_out_mla):
    bp, seq, d = x_prompt.shape
    bs, ls, _ = x_sample.shape
    n_a = w_in_conv.shape[0]
    n_b = w_in_mla.shape[0]
    c_dim = conv_w.shape[2]
    kv_lora = kv_lat_norm_g.shape[0]
    q_lora = q_norm_g.shape[1]
    past_len = page_table.shape[1] * cache_ckv.shape[1]

    row = lambda v: v.reshape(1, -1).astype(f32)
    w_in_conv_b = w_in_conv.astype(bf16)
    w_out_conv_b = w_out_conv.astype(bf16)
    w_in_mla_b = w_in_mla.astype(bf16)
    w_out_mla_b = w_out_mla.astype(bf16)
    w_dkv_pad = jnp.concatenate([w_dkv, jnp.zeros((d, LANES - QK_ROPE), w_dkv.dtype)], axis=-1).astype(bf16)
    uq = w_uq.reshape(n_b, q_lora, N_HEADS, QK_NOPE + QK_ROPE)
    w_uq_pad = _head_blocks(uq[..., QK_NOPE:], uq[..., :QK_NOPE]).astype(bf16)
    w_uk_blk = _head_blocks(jnp.zeros((kv_lora, N_HEADS, QK_ROPE), w_uk.dtype), w_uk)
    w_uk_pad = w_uk_blk.astype(bf16)
    w_ukt_pad = jnp.transpose(w_uk_blk.reshape(kv_lora, N_HEADS, LANES), (1, 2, 0)).astype(bf16)
    uv = jnp.transpose(w_uv, (1, 0, 2))
    zv = jnp.zeros_like(uv)
    even = (jnp.arange(N_HEADS) % 2 == 0)[:, None, None]
    w_uv_pad = jnp.concatenate([jnp.where(even, uv, zv), jnp.where(even, zv, uv)], axis=-1).astype(bf16)
    w_uv_blk = jnp.transpose(w_uv_pad, (1, 0, 2)).reshape(kv_lora, N_HEADS * LANES)

    def trunk_a(x, inits, *, nseq, tile, shift):
        states = []
        for l in range(n_a):
            x, st = _conv_layer(x, inits[l], row(pre_norm_g[l]), row(post_norm_g[l]), w_in_conv_b[l],
                                conv_w[l], w_out_conv_b[l], nseq=nseq, tile=tile, shift=shift)
            states.append(st)
        return x, states

    zero_init = jnp.zeros((1, SUBLANES, c_dim), f32)
    xm, meta_states = trunk_a(meta_tokens.astype(f32), [zero_init] * n_a, nseq=1, tile=N_META, shift=1)

    tile_p = min(ROW_TILE, seq)
    inits_p = [jnp.concatenate([jnp.zeros((1, SUBLANES - 2, c_dim), f32), st], axis=1) for st in meta_states]
    xp, prompt_states = trunk_a(x_prompt.reshape(bp * seq, d), inits_p, nseq=bp, tile=tile_p, shift=1)
    t_pos = N_META + seq
    t_pad = -(-t_pos // tile_p) * tile_p
    x_pos = jnp.concatenate([jnp.broadcast_to(xm[None], (bp, N_META, d)), xp.reshape(bp, seq, d),
                             jnp.zeros((bp, t_pad - t_pos, d), f32)], axis=1)
    tabs_pos = _rope_tables(jnp.arange(t_pad, dtype=jnp.int32), 1.0)
    ckv_prompt, krope_prompt, k_p, v_p = _latent(x_pos, row(kv_norm_g), w_dkv_pad, row(kv_lat_norm_g), tabs_pos,
                                                 w_uk_pad, w_uv_blk, t_out=t_pos, tile=tile_p)
    tabs_q = _rope_tables(N_META + jnp.arange(seq, dtype=jnp.int32), SOFTMAX_SCALE * LOG2_E)
    attn_tile = min(ATTN_TILE, seq)
    for j in range(n_b):
        l = n_a + j
        q, z = _mla_query(xp, row(pre_norm_g[l]), w_in_mla_b[j], row(q_norm_g[j]), w_uq_pad[j], tabs_q,
                          nseq=bp, tile=tile_p)
        o = _prompt_attn(q.reshape(bp, seq, -1), k_p, v_p, tile=attn_tile)
        xp = _mla_out(o.reshape(bp * seq, -1), z, xp, w_out_mla_b[j], row(post_norm_g[l]), tile=tile_p)
    y_prompt = xp.reshape(bp, seq, d)
    conv_prompt = jnp.stack([st for st in prompt_states])

    rs = bs * ls
    xs = jnp.transpose(x_sample, (1, 0, 2)).reshape(rs, d)
    inits_s = [jnp.transpose(state_conv[l], (1, 0, 2)).reshape(1, (CONV_WIDTH - 1) * bs, c_dim) for l in range(n_a)]
    xs, sample_states = trunk_a(xs, inits_s, nseq=1, tile=rs, shift=bs)
    conv_sample = jnp.stack([jnp.transpose(st.reshape(CONV_WIDTH - 1, bs, c_dim), (1, 0, 2)) for st in sample_states])
    xs = jnp.transpose(xs.reshape(ls, bs, d), (1, 0, 2)).reshape(rs, d)
    pos_s = jnp.tile(past_len + jnp.arange(ls, dtype=jnp.int32), bs)
    c_s, kr_s = _latent(xs[None], row(kv_norm_g), w_dkv_pad, row(kv_lat_norm_g), _rope_tables(pos_s, 1.0),
                        t_out=rs, tile=rs)
    lpad = 16
    c_new = jnp.pad(c_s.reshape(bs, ls, kv_lora), ((0, 0), (0, lpad - ls), (0, 0)))
    kr_new = jnp.pad(kr_s.reshape(bs, ls, QK_ROPE), ((0, 0), (0, lpad - ls), (0, 0)))
    tabs_qs = _rope_tables(pos_s, SOFTMAX_SCALE * LOG2_E)
    cache_krope_t = jnp.swapaxes(cache_krope, 1, 2)
    for j in range(n_b):
        l = n_a + j
        q_lat, q_rope, z = _mla_query(xs, row(pre_norm_g[l]), w_in_mla_b[j], row(q_norm_g[j]), w_uq_pad[j],
                                      tabs_qs, w_ukt_pad, nseq=1, tile=rs)
        o_lat = _sample_attn(page_table, q_lat, q_rope, c_new, kr_new, cache_ckv, cache_krope_t)
        xs = _mla_out(o_lat, z, xs, w_out_mla_b[j], row(post_norm_g[l]), w_uv_pad, tile=rs)
    y_sample = xs.reshape(bs, ls, d)
    ckv_sample = c_s.reshape(bs, ls, kv_lora)
    krope_sample = kr_s.reshape(bs, ls, QK_ROPE)

    return (y_prompt, y_sample, ckv_prompt, krope_prompt, conv_prompt, ckv_sample, krope_sample, conv_sample)
```

```python
import functools

import jax
import jax.numpy as jnp
from jax import lax
from jax.experimental import pallas as pl
from jax.experimental.pallas import tpu as pltpu

N_META = 16
N_HEADS = 16
QK_NOPE = 64
QK_ROPE = 32
V_HEAD = 64
ROPE_THETA = 10000.0
RMS_EPS = 1e-6
CONV_WIDTH = 3
SOFTMAX_SCALE = (QK_NOPE + QK_ROPE) ** -0.5
LOG2_E = 1.4426950408889634

LANES = 128
SUBLANES = 8
HALF_ROPE = QK_ROPE // 2
ROPE_LANE0 = QK_ROPE
VMEM_LIMIT = 56 * 1024 * 1024

ROW_TILE = 512
ATTN_TILE = 1024
ATTN_HEADS_PER_STEP = 4
SAMPLE_ATTN_CHUNKS = 4

bf16 = jnp.bfloat16
f32 = jnp.float32


def _rms(x, g):
    return x * lax.rsqrt(jnp.mean(x * x, axis=-1, keepdims=True) + RMS_EPS) * g


def _dot(a, b):
    return jnp.dot(a, b, preferred_element_type=f32)


def _dot_nt(a, b):
    return lax.dot_general(a, b, (((1,), (1,)), ((), ())), preferred_element_type=f32)


def _rope_block(blk, cos, sin):
    return blk * cos + pltpu.roll(blk, HALF_ROPE, 1) * sin


def _params(*sem):
    return pltpu.CompilerParams(dimension_semantics=sem, vmem_limit_bytes=VMEM_LIMIT)


def _const_spec(shape):
    nd = len(shape)
    return pl.BlockSpec(shape, lambda *_: (0,) * nd)


def _conv_layer_kernel(x_ref, init_ref, pre_g_ref, post_g_ref, w_in_ref, cw_ref, w_out_ref,
                       xo_ref, st_ref, vbuf, *, tile, off, shift, c_dim):
    i = pl.program_id(1)

    @pl.when(i == 0)
    def _():
        vbuf[0:off, :] = init_ref[0]

    x = x_ref[...]
    h = _rms(x, pre_g_ref[...]).astype(bf16)

    def proj(k):
        return _dot(h, w_in_ref[:, k * c_dim:(k + 1) * c_dim])

    vbuf[off:off + tile, :] = proj(1) * proj(2)
    cw = cw_ref[...]
    y = cw[0:1] * vbuf[off - 2 * shift:off - 2 * shift + tile, :]
    y = y + cw[1:2] * vbuf[off - shift:off - shift + tile, :]
    y = y + cw[2:3] * vbuf[off:off + tile, :]
    z = proj(3)
    g = (proj(0) * y * jax.nn.silu(z)).astype(bf16)
    m = _dot(g, w_out_ref[...])
    xo_ref[...] = x + _rms(m, post_g_ref[...])

    @pl.when(i == pl.num_programs(1) - 1)
    def _():
        st_ref[0] = vbuf[off + tile - 2 * shift:off + tile, :]

    vbuf[0:off, :] = vbuf[tile:tile + off, :]


def _conv_layer(x, init, pre_g, post_g, w_in, cw, w_out, *, nseq, tile, shift):
    rows, d = x.shape
    c_dim = cw.shape[1]
    t = rows // nseq
    nt = t // tile
    off = init.shape[1]
    ninit = init.shape[0]
    kern = functools.partial(_conv_layer_kernel, tile=tile, off=off, shift=shift, c_dim=c_dim)
    return pl.pallas_call(
        kern,
        grid=(nseq, nt),
        in_specs=[
            pl.BlockSpec((tile, d), lambda b, i: (b * nt + i, 0)),
            pl.BlockSpec((1, off, c_dim), (lambda b, i: (b, 0, 0)) if ninit > 1 else (lambda b, i: (0, 0, 0))),
            _const_spec((1, d)), _const_spec((1, d)),
            _const_spec(w_in.shape), _const_spec(cw.shape), _const_spec(w_out.shape),
        ],
        out_specs=[
            pl.BlockSpec((tile, d), lambda b, i: (b * nt + i, 0)),
            pl.BlockSpec((1, 2 * shift, c_dim), lambda b, i: (b, 0, 0)),
        ],
        out_shape=[jax.ShapeDtypeStruct((rows, d), f32),
                   jax.ShapeDtypeStruct((nseq, 2 * shift, c_dim), f32)],
        scratch_shapes=[pltpu.VMEM((off + tile, c_dim), f32)],
        compiler_params=_params("arbitrary", "arbitrary"),
        name="conv_layer",
    )(x, init, pre_g, post_g, w_in, cw, w_out)


def _latent_kernel(*refs, kv_lora, with_kv):
    if with_kv:
        (x_ref, g_ref, w_dkv_ref, lat_g_ref, cos_ref, sin_ref, w_uk_ref, w_uv_ref,
         c_ref, kr_ref, k_ref, v_ref) = refs
    else:
        x_ref, g_ref, w_dkv_ref, lat_g_ref, cos_ref, sin_ref, c_ref, kr_ref = refs
    xn = _rms(x_ref[...], g_ref[...]).astype(bf16)
    ckr = _dot(xn, w_dkv_ref[...])
    c = _rms(ckr[:, :kv_lora], lat_g_ref[...])
    krb = _rope_block(ckr[:, kv_lora:kv_lora + LANES], cos_ref[...], sin_ref[...])
    c_ref[...] = c
    kr_ref[...] = krb[:, ROPE_LANE0:ROPE_LANE0 + QK_ROPE]
    if with_kv:
        cb = c.astype(bf16)
        kn = _dot(cb, w_uk_ref[...])
        for h in range(N_HEADS):
            k_ref[:, h * LANES:(h + 1) * LANES] = (kn[:, h * LANES:(h + 1) * LANES] + krb).astype(bf16)
        pos = lax.broadcasted_iota(jnp.int32, (1, N_HEADS * LANES), 1) % (2 * LANES)
        ones_lanes = jnp.where(pos < V_HEAD, 0.0, jnp.where(pos < V_HEAD + LANES, 1.0, 0.0)).astype(f32)
        v_ref[...] = (_dot(cb, w_uv_ref[...]) + ones_lanes).astype(bf16)


def _latent(x, g, w_dkv_pad, lat_g, tabs, w_uk_pad=None, w_uv=None, *, t_out, tile):
    nseq, t_in, d = x.shape
    kv_lora = lat_g.shape[1]
    nt = t_in // tile
    with_kv = w_uk_pad is not None
    row_spec = lambda w: pl.BlockSpec((None, tile, w), lambda b, i: (b, i, 0))
    tab_spec = pl.BlockSpec((tile, LANES), lambda b, i: (i, 0))
    in_specs = [row_spec(d), _const_spec((1, d)), _const_spec(w_dkv_pad.shape), _const_spec((1, kv_lora)),
                tab_spec, tab_spec]
    args = [x, g, w_dkv_pad, lat_g, *tabs]
    out_specs = [row_spec(kv_lora), row_spec(QK_ROPE)]
    out_shape = [jax.ShapeDtypeStruct((nseq, t_out, kv_lora), f32), jax.ShapeDtypeStruct((nseq, t_out, QK_ROPE), f32)]
    if with_kv:
        in_specs += [_const_spec(w_uk_pad.shape), _const_spec(w_uv.shape)]
        args += [w_uk_pad, w_uv]
        out_specs += [row_spec(N_HEADS * LANES), row_spec(N_HEADS * LANES)]
        out_shape += [jax.ShapeDtypeStruct((nseq, t_out, N_HEADS * LANES), bf16),
                      jax.ShapeDtypeStruct((nseq, t_out, N_HEADS * LANES), bf16)]
    return pl.pallas_call(
        functools.partial(_latent_kernel, kv_lora=kv_lora, with_kv=with_kv),
        grid=(nseq, nt), in_specs=in_specs, out_specs=out_specs, out_shape=out_shape,
        compiler_params=_params("arbitrary", "arbitrary"),
        name="latent_kv" if with_kv else "latent",
    )(*args)


def _mla_query_kernel(*refs, q_lora, absorbed, rows):
    if absorbed:
        (x_ref, pre_g_ref, w_in_ref, qg_ref, w_uq_ref, cos_ref, sin_ref, w_ukt_ref,
         ql_ref, qr_ref, z_ref) = refs
    else:
        x_ref, pre_g_ref, w_in_ref, qg_ref, w_uq_ref, cos_ref, sin_ref, q_ref, z_ref = refs
    h = _rms(x_ref[...], pre_g_ref[...]).astype(bf16)
    q_lat = _dot(h, w_in_ref[:, :q_lora])
    z_ref[...] = _dot(h, w_in_ref[:, q_lora:]).astype(z_ref.dtype)
    qn = _rms(q_lat, qg_ref[...]).astype(bf16)
    q = _dot(qn, w_uq_ref[...])
    cos, sin = cos_ref[...], sin_ref[...]
    for hd in range(N_HEADS):
        blk = _rope_block(q[:, hd * LANES:(hd + 1) * LANES], cos, sin)
        if absorbed:
            ql = _dot(blk.astype(bf16), w_ukt_ref[hd])
            for c in range(ql.shape[1] // LANES):
                ql_ref[c, pl.ds(hd, rows, stride=N_HEADS), :] = ql[:, c * LANES:(c + 1) * LANES]
            qr_ref[pl.ds(hd, rows, stride=N_HEADS), :] = blk
        else:
            q_ref[:, hd * LANES:(hd + 1) * LANES] = blk.astype(bf16)


def _mla_query(x, pre_g, w_in, qg, w_uq_pad, tabs, w_ukt_pad=None, *, nseq, tile):
    rows, d = x.shape
    q_lora = qg.shape[1]
    z_dim = w_in.shape[1] - q_lora
    nt = rows // nseq // tile
    absorbed = w_ukt_pad is not None
    row_spec = lambda w: pl.BlockSpec((tile, w), lambda b, i: (b * nt + i, 0))
    tab_spec = pl.BlockSpec((tile, LANES), lambda b, i: (i, 0))
    in_specs = [row_spec(d), _const_spec((1, d)), _const_spec(w_in.shape), _const_spec((1, q_lora)),
                _const_spec(w_uq_pad.shape), tab_spec, tab_spec]
    args = [x, pre_g, w_in, qg, w_uq_pad, *tabs]
    if absorbed:
        assert nseq == 1 and nt == 1
        kv_lora = w_ukt_pad.shape[2]
        in_specs.append(_const_spec(w_ukt_pad.shape))
        args.append(w_ukt_pad)
        ql_shape = (kv_lora // LANES, rows * N_HEADS, LANES)
        out_specs = [_const_spec(ql_shape), _const_spec((rows * N_HEADS, LANES)), row_spec(z_dim)]
        out_shape = [jax.ShapeDtypeStruct(ql_shape, f32),
                     jax.ShapeDtypeStruct((rows * N_HEADS, LANES), f32),
                     jax.ShapeDtypeStruct((rows, z_dim), bf16)]
    else:
        out_specs = [row_spec(N_HEADS * LANES), row_spec(z_dim)]
        out_shape = [jax.ShapeDtypeStruct((rows, N_HEADS * LANES), bf16),
                     jax.ShapeDtypeStruct((rows, z_dim), bf16)]
    return pl.pallas_call(
        functools.partial(_mla_query_kernel, q_lora=q_lora, absorbed=absorbed, rows=rows),
        grid=(nseq, nt), in_specs=in_specs, out_specs=out_specs, out_shape=out_shape,
        compiler_params=_params("arbitrary", "arbitrary"),
        name="mla_query_absorbed" if absorbed else "mla_query",
    )(*args)


def _prompt_attn_kernel(q_ref, k_ref, v_ref, o_ref, *, tile):
    i = pl.program_id(2)
    neg = jnp.finfo(f32).min
    heads = range(q_ref.shape[2] // LANES)
    n_sub = 2
    sub = tile // n_sub

    def keys(hh, start, size):
        return k_ref[0, pl.ds(start, size), hh * LANES:(hh + 1) * LANES]

    def values(hh, start, size):
        return v_ref[0, pl.ds(start, size), hh * LANES:(hh + 1) * LANES]

    d0 = pl.multiple_of(i * tile, tile)
    head_rows = sub - N_META

    def init_rows(hh, r):
        q = q_ref[0, r * sub:(r + 1) * sub, hh * LANES:(hh + 1) * LANES]
        nk = (r + 1) * sub
        row = lax.broadcasted_iota(jnp.int32, (sub, nk), 0) + (r * sub + N_META)
        col = lax.broadcasted_iota(jnp.int32, (sub, nk), 1)
        s = jnp.where(col <= row, _dot_nt(q, keys(hh, d0, nk)), neg)
        m = jnp.max(s, axis=-1, keepdims=True)
        acc = _dot(jnp.exp2(s - m).astype(bf16), values(hh, d0, nk))
        sc = _dot_nt(q[head_rows:], keys(hh, d0 + nk, N_META))
        rowc = lax.broadcasted_iota(jnp.int32, (N_META, N_META), 0)
        colc = lax.broadcasted_iota(jnp.int32, (N_META, N_META), 1)
        sc = jnp.where(colc <= rowc, sc, neg)
        m_c = jnp.max(sc, axis=-1, keepdims=True)
        acc_c = _dot(jnp.exp2(sc - m_c).astype(bf16), values(hh, d0 + nk, N_META))
        m_tail = jnp.maximum(m[head_rows:], m_c)
        acc_tail = jnp.exp2(m[head_rows:] - m_tail) * acc[head_rows:] + jnp.exp2(m_c - m_tail) * acc_c
        m_b = jnp.concatenate([jnp.broadcast_to(m[:head_rows], (head_rows, LANES)),
                               jnp.broadcast_to(m_tail, (N_META, LANES))], axis=0)
        return m_b, jnp.concatenate([acc[:head_rows], acc_tail], axis=0)

    carry = []
    for hh in heads:
        parts = [init_rows(hh, r) for r in range(n_sub)]
        m_rows = jnp.concatenate([pt[0] for pt in parts], axis=0)
        carry += [jnp.max(m_rows, axis=-1, keepdims=True), jnp.concatenate([pt[1] for pt in parts], axis=0)]

    qs = [q_ref[0, :, hh * LANES:(hh + 1) * LANES] for hh in heads]

    def step(carry, start, size):
        out = []
        for hh in heads:
            m, acc = carry[2 * hh], carry[2 * hh + 1]
            s = _dot_nt(qs[hh], keys(hh, start, size))
            m_new = jnp.maximum(m, jnp.max(s, axis=-1, keepdims=True))
            alpha = jnp.exp2(m - m_new)
            p = jnp.exp2(s - m_new)
            out += [m_new, alpha * acc + _dot(p.astype(bf16), values(hh, start, size))]
        return tuple(out)

    carry = lax.fori_loop(0, i // 2, lambda j, c: step(c, pl.multiple_of(j * 2 * tile, 2 * tile), 2 * tile),
                          tuple(carry))
    carry = lax.cond(i % 2 == 1, lambda c: step(c, pl.multiple_of((i - 1) * tile, tile), tile), lambda c: c, carry)
    lane = lax.broadcasted_iota(jnp.int32, (tile, LANES), 1)
    for g in range(len(heads) // 2):
        a_even, a_odd = carry[4 * g + 1], carry[4 * g + 3]
        num = jnp.where(lane < V_HEAD, a_even, a_odd)
        den = pltpu.roll(jnp.where(lane < V_HEAD, a_odd, a_even), V_HEAD, 1)
        o_ref[0, :, g * LANES:(g + 1) * LANES] = (num / den).astype(o_ref.dtype)


def _prompt_attn(q, k, v, *, tile):
    b, t, _ = q.shape
    tk = k.shape[1]
    nq = t // tile
    hps = ATTN_HEADS_PER_STEP
    return pl.pallas_call(
        functools.partial(_prompt_attn_kernel, tile=tile),
        grid=(b, N_HEADS // hps, nq),
        in_specs=[
            pl.BlockSpec((1, tile, hps * LANES), lambda b, g, i: (b, i, g)),
            pl.BlockSpec((1, tk, hps * LANES), lambda b, g, i: (b, 0, g)),
            pl.BlockSpec((1, tk, hps * LANES), lambda b, g, i: (b, 0, g)),
        ],
        out_specs=pl.BlockSpec((1, tile, hps * V_HEAD), lambda b, g, i: (b, i, g)),
        out_shape=jax.ShapeDtypeStruct((b, t, N_HEADS * V_HEAD), bf16),
        compiler_params=_params("arbitrary", "arbitrary", "arbitrary"),
        name="prompt_attn",
    )(q, k, v)


def _sample_attn_kernel(pt_ref, ql_ref, qr_ref, cn_ref, krn_ref, ckv_hbm, krt_hbm, o_ref,
                        kv_land, kr_land, kb16, kr16, sem_kv, sem_kr, *, n_pages, page, n_chunks):
    b = pl.program_id(0)
    slot = lax.rem(b, 2)

    def page_copies(req, slot_, p):
        pg = pt_ref[req * n_pages + p]
        return (pltpu.make_async_copy(ckv_hbm.at[pg], kv_land.at[slot_, p], sem_kv.at[slot_]),
                pltpu.make_async_copy(krt_hbm.at[pg], kr_land.at[slot_, p], sem_kr.at[slot_]))

    def wait_slot(slot_):
        pltpu.make_async_copy(ckv_hbm.at[pl.ds(0, n_pages)], kv_land.at[slot_], sem_kv.at[slot_]).wait()
        pltpu.make_async_copy(krt_hbm.at[pl.ds(0, n_pages)], kr_land.at[slot_], sem_kr.at[slot_]).wait()

    @pl.when(b == 0)
    def _():
        def body(p, c):
            for cp in page_copies(0, 0, p):
                cp.start()
            return c
        lax.fori_loop(0, n_pages, body, 0)

    wait_slot(slot)
    nxt = jnp.minimum(b + 1, pl.num_programs(0) - 1)
    for p in range(n_pages):
        for cp in page_copies(nxt, 1 - slot, p):
            cp.start(priority=p % 2)

    n_half = ql_ref.shape[0]
    ql = jnp.concatenate([ql_ref[c] for c in range(n_half)], axis=-1).astype(bf16)
    qr = qr_ref[:, ROPE_LANE0:ROPE_LANE0 + QK_ROPE].astype(bf16)
    nq = ql.shape[0]

    ppc = n_pages // n_chunks
    parts = []
    for c in range(n_chunks):
        for p in range(c * ppc, (c + 1) * ppc):
            kb16[p * page:(p + 1) * page, :] = kv_land[slot, p].astype(bf16)
            kr16[:, p * page:(p + 1) * page] = kr_land[slot, p].astype(bf16)
        kb = kb16[c * ppc * page:(c + 1) * ppc * page, :]
        half = ppc * page // 2
        s = jnp.concatenate([_dot_nt(ql, kb[:half]), _dot_nt(ql, kb[half:])], axis=-1)
        s = s + _dot(qr, kr16[:, c * ppc * page:(c + 1) * ppc * page])
        m_c = jnp.max(s, axis=-1, keepdims=True)
        p_c = jnp.exp2(s - m_c)
        pb = p_c.astype(bf16)
        a_c = _dot(pb[:, :half], kb[:half]) + _dot(pb[:, half:], kb[half:])
        parts.append((m_c, jnp.sum(p_c, axis=-1, keepdims=True), a_c))
    cn = cn_ref[0].astype(bf16)
    krn = krn_ref[0].astype(bf16)
    npad = cn.shape[0]
    s_new = _dot_nt(ql, cn) + _dot_nt(qr, krn)
    row = lax.broadcasted_iota(jnp.int32, (nq, npad), 0)
    col = lax.broadcasted_iota(jnp.int32, (nq, npad), 1)
    s_new = jnp.where(col * N_HEADS <= row, s_new, jnp.finfo(f32).min)
    m_n = jnp.max(s_new, axis=-1, keepdims=True)
    p_n = jnp.exp2(s_new - m_n)
    parts.append((m_n, jnp.sum(p_n, axis=-1, keepdims=True), _dot(p_n.astype(bf16), cn)))

    m = functools.reduce(jnp.maximum, [pt_[0] for pt_ in parts])
    l = sum(jnp.exp2(m_c - m) * l_c for m_c, l_c, _ in parts)
    acc = sum(jnp.exp2(m_c - m) * a_c for m_c, _, a_c in parts)
    o = acc / l
    for c in range(n_half):
        o_ref[c] = o[:, c * LANES:(c + 1) * LANES]

    @pl.when(b == pl.num_programs(0) - 1)
    def _():
        wait_slot(1 - slot)


def _sample_attn(page_table, q_lat, q_rope, c_new, kr_new, cache_ckv, cache_krope_t):
    nreq, n_pages = page_table.shape
    _, page, kv_lora = cache_ckv.shape
    n_half = q_lat.shape[0]
    nq = q_lat.shape[1] // nreq
    lpad = c_new.shape[1]
    grid_spec = pltpu.PrefetchScalarGridSpec(
        num_scalar_prefetch=1, grid=(nreq,),
        in_specs=[
            pl.BlockSpec((n_half, nq, LANES), lambda b, pt: (0, b, 0)),
            pl.BlockSpec((nq, LANES), lambda b, pt: (b, 0)),
            pl.BlockSpec((1, lpad, kv_lora), lambda b, pt: (b, 0, 0)),
            pl.BlockSpec((1, lpad, QK_ROPE), lambda b, pt: (b, 0, 0)),
            pl.BlockSpec(memory_space=pl.ANY),
            pl.BlockSpec(memory_space=pl.ANY),
        ],
        out_specs=pl.BlockSpec((n_half, nq, LANES), lambda b, pt: (0, b, 0)),
        scratch_shapes=[
            pltpu.VMEM((2, n_pages, page, kv_lora), f32),
            pltpu.VMEM((2, n_pages, QK_ROPE, page), f32),
            pltpu.VMEM((n_pages * page, kv_lora), bf16),
            pltpu.VMEM((QK_ROPE, n_pages * page), bf16),
            pltpu.SemaphoreType.DMA((2,)),
            pltpu.SemaphoreType.DMA((2,)),
        ],
    )
    return pl.pallas_call(
        functools.partial(_sample_attn_kernel, n_pages=n_pages, page=page,
                          n_chunks=min(SAMPLE_ATTN_CHUNKS, n_pages)),
        grid_spec=grid_spec,
        out_shape=jax.ShapeDtypeStruct((n_half, nreq * nq, LANES), f32),
        compiler_params=_params("arbitrary"),
        name="sample_attn",
    )(page_table.reshape(-1), q_lat, q_rope, c_new, kr_new, cache_ckv, cache_krope_t)


def _mla_out_kernel(*refs, absorbed, rows):
    if absorbed:
        ol_ref, w_uv_ref, z_ref, x_ref, w_out_ref, post_g_ref, xo_ref = refs
        def head_rows(hd):
            halves = [ol_ref[c, pl.ds(hd, rows, stride=N_HEADS), :] for c in range(ol_ref.shape[0])]
            return jnp.concatenate(halves, axis=-1).astype(bf16)

        parts = []
        for g in range(N_HEADS // 2):
            parts.append(_dot(head_rows(2 * g), w_uv_ref[2 * g]) + _dot(head_rows(2 * g + 1), w_uv_ref[2 * g + 1]))
        o = jnp.concatenate(parts, axis=-1)
    else:
        o_ref, z_ref, x_ref, w_out_ref, post_g_ref, xo_ref = refs
        o = o_ref[...]
    g = (o.astype(f32) * jax.nn.silu(z_ref[...].astype(f32))).astype(bf16)
    m = _dot(g, w_out_ref[...])
    xo_ref[...] = x_ref[...] + _rms(m, post_g_ref[...])


def _mla_out(o, z, x, w_out, post_g, w_uv_pad=None, *, tile):
    rows, d = x.shape
    z_dim = z.shape[1]
    nt = rows // tile
    absorbed = w_uv_pad is not None
    row_spec = lambda w: pl.BlockSpec((tile, w), lambda i: (i, 0))
    if absorbed:
        assert nt == 1
        in_specs = [_const_spec(o.shape), _const_spec(w_uv_pad.shape)]
        args = [o, w_uv_pad]
    else:
        in_specs = [row_spec(z_dim)]
        args = [o]
    in_specs += [row_spec(z_dim), row_spec(d), _const_spec(w_out.shape), _const_spec((1, d))]
    args += [z, x, w_out, post_g]
    return pl.pallas_call(
        functools.partial(_mla_out_kernel, absorbed=absorbed, rows=rows),
        grid=(nt,), in_specs=in_specs, out_specs=row_spec(d),
        out_shape=jax.ShapeDtypeStruct((rows, d), f32),
        compiler_params=_params("arbitrary"),
        name="mla_out_absorbed" if absorbed else "mla_out",
    )(*args)


def _rope_tables(pos, scale):
    inv = ROPE_THETA ** (-jnp.arange(HALF_ROPE, dtype=f32) / HALF_ROPE)
    ang = pos.astype(f32)[:, None] * inv[None, :]
    cos, sin = jnp.cos(ang), jnp.sin(ang)
    r = pos.shape[0]
    zeros = lambda w: jnp.zeros((r, w), f32)
    c = jnp.concatenate([zeros(ROPE_LANE0), cos, cos, jnp.ones((r, QK_NOPE), f32)], axis=-1)
    s = jnp.concatenate([zeros(ROPE_LANE0), -sin, sin, zeros(QK_NOPE)], axis=-1)
    return c * scale, s * scale


def _head_blocks(w_rope, w_nope):
    blk = jnp.concatenate([w_rope, w_rope, w_nope], axis=-1)
    return blk.reshape(blk.shape[:-2] + (N_HEADS * LANES,))


def kernel(x_prompt, x_sample, cache_ckv, cache_krope, state_conv, page_table, meta_tokens,
           pre_norm_g, post_norm_g, w_in_conv, conv_w, w_out_conv, kv_norm_g, w_dkv,
           kv_lat_norm_g, w_uk, w_uv, w_in_mla, q_norm_g, w_uq, w_out_mla):
    bp, seq, d = x_prompt.shape
    bs, ls, _ = x_sample.shape
    n_a = w_in_conv.shape[0]
    n_b = w_in_mla.shape[0]
    c_dim = conv_w.shape[2]
    kv_lora = kv_lat_norm_g.shape[0]
    q_lora = q_norm_g.shape[1]
    past_len = page_table.shape[1] * cache_ckv.shape[1]

    row = lambda v: v.reshape(1, -1).astype(f32)
    w_in_conv_b = w_in_conv.astype(bf16)
    w_out_conv_b = w_out_conv.astype(bf16)
    w_in_mla_b = w_in_mla.astype(bf16)
    w_out_mla_b = w_out_mla.astype(bf16)
    w_dkv_pad = jnp.concatenate([w_dkv, w_dkv[:, kv_lora:], jnp.zeros((d, QK_NOPE), w_dkv.dtype)],
                                axis=-1).astype(bf16)
    uq = w_uq.reshape(n_b, q_lora, N_HEADS, QK_NOPE + QK_ROPE)
    w_uq_pad = _head_blocks(uq[..., QK_NOPE:], uq[..., :QK_NOPE]).astype(bf16)
    w_uk_blk = _head_blocks(jnp.zeros((kv_lora, N_HEADS, QK_ROPE), w_uk.dtype), w_uk)
    w_uk_pad = w_uk_blk.astype(bf16)
    w_ukt_pad = jnp.transpose(w_uk_blk.reshape(kv_lora, N_HEADS, LANES), (1, 2, 0)).astype(bf16)
    uv = jnp.transpose(w_uv, (1, 0, 2))
    zv = jnp.zeros_like(uv)
    even = (jnp.arange(N_HEADS) % 2 == 0)[:, None, None]
    w_uv_pad = jnp.concatenate([jnp.where(even, uv, zv), jnp.where(even, zv, uv)], axis=-1).astype(bf16)
    w_uv_blk = jnp.transpose(w_uv_pad, (1, 0, 2)).reshape(kv_lora, N_HEADS * LANES)

    def trunk_a(x, inits, *, nseq, tile, shift):
        states = []
        for l in range(n_a):
            x, st = _conv_layer(x, inits[l], row(pre_norm_g[l]), row(post_norm_g[l]), w_in_conv_b[l],
                                conv_w[l], w_out_conv_b[l], nseq=nseq, tile=tile, shift=shift)
            states.append(st)
        return x, states

    zero_init = jnp.zeros((1, SUBLANES, c_dim), f32)
    xm, meta_states = trunk_a(meta_tokens.astype(f32), [zero_init] * n_a, nseq=1, tile=N_META, shift=1)

    tile_p = min(ROW_TILE, seq)
    inits_p = [jnp.concatenate([jnp.zeros((1, SUBLANES - 2, c_dim), f32), st], axis=1) for st in meta_states]
    xp, prompt_states = trunk_a(x_prompt.reshape(bp * seq, d), inits_p, nseq=bp, tile=tile_p, shift=1)
    t_pos = N_META + seq
    t_pad = -(-t_pos // tile_p) * tile_p
    x_pos = jnp.concatenate([jnp.broadcast_to(xm[None], (bp, N_META, d)), xp.reshape(bp, seq, d),
                             jnp.zeros((bp, t_pad - t_pos, d), f32)], axis=1)
    tabs_pos = _rope_tables(jnp.arange(t_pad, dtype=jnp.int32), 1.0)
    ckv_prompt, krope_prompt, k_p, v_p = _latent(x_pos, row(kv_norm_g), w_dkv_pad, row(kv_lat_norm_g), tabs_pos,
                                                 w_uk_pad, w_uv_blk, t_out=t_pos, tile=tile_p)
    tabs_q = _rope_tables(N_META + jnp.arange(seq, dtype=jnp.int32), SOFTMAX_SCALE * LOG2_E)
    attn_tile = min(ATTN_TILE, seq)
    for j in range(n_b):
        l = n_a + j
        q, z = _mla_query(xp, row(pre_norm_g[l]), w_in_mla_b[j], row(q_norm_g[j]), w_uq_pad[j], tabs_q,
                          nseq=bp, tile=tile_p)
        o = _prompt_attn(q.reshape(bp, seq, -1), k_p, v_p, tile=attn_tile)
        xp = _mla_out(o.reshape(bp * seq, -1), z, xp, w_out_mla_b[j], row(post_norm_g[l]), tile=tile_p)
    y_prompt = xp.reshape(bp, seq, d)
    conv_prompt = jnp.stack([st for st in prompt_states])

    rs = bs * ls
    xs = jnp.transpose(x_sample, (1, 0, 2)).reshape(rs, d)
    inits_s = [jnp.transpose(state_conv[l], (1, 0, 2)).reshape(1, (CONV_WIDTH - 1) * bs, c_dim) for l in range(n_a)]
    xs, sample_states = trunk_a(xs, inits_s, nseq=1, tile=rs, shift=bs)
    conv_sample = jnp.stack([jnp.transpose(st.reshape(CONV_WIDTH - 1, bs, c_dim), (1, 0, 2)) for st in sample_states])
    xs = jnp.transpose(xs.reshape(ls, bs, d), (1, 0, 2)).reshape(rs, d)
    pos_s = jnp.tile(past_len + jnp.arange(ls, dtype=jnp.int32), bs)
    c_s, kr_s = _latent(xs[None], row(kv_norm_g), w_dkv_pad, row(kv_lat_norm_g), _rope_tables(pos_s, 1.0),
                        t_out=rs, tile=rs)
    lpad = 16
    c_new = jnp.pad(c_s.reshape(bs, ls, kv_lora), ((0, 0), (0, lpad - ls), (0, 0)))
    kr_new = jnp.pad(kr_s.reshape(bs, ls, QK_ROPE), ((0, 0), (0, lpad - ls), (0, 0)))
    tabs_qs = _rope_tables(pos_s, SOFTMAX_SCALE * LOG2_E)
    cache_krope_t = jnp.swapaxes(cache_krope, 1, 2)
    for j in range(n_b):
        l = n_a + j
        q_lat, q_rope, z = _mla_query(xs, row(pre_norm_g[l]), w_in_mla_b[j], row(q_norm_g[j]), w_uq_pad[j],
                                      tabs_qs, w_ukt_pad, nseq=1, tile=rs)
        o_lat = _sample_attn(page_table, q_lat, q_rope, c_new, kr_new, cache_ckv, cache_krope_t)
        xs = _mla_out(o_lat, z, xs, w_out_mla_b[j], row(post_norm_g[l]), w_uv_pad, tile=rs)
    y_sample = xs.reshape(bs, ls, d)
    ckv_sample = c_s.reshape(bs, ls, kv_lora)
    krope_sample = kr_s.reshape(bs, ls, QK_ROPE)

    return (y_prompt, y_sample, ckv_prompt, krope_prompt, conv_prompt, ckv_sample, krope_sample, conv_sample)
```

```python
import functools

import jax
import jax.numpy as jnp
from jax import lax
from jax.experimental import pallas as pl
from jax.experimental.pallas import tpu as pltpu

N_META = 16
N_HEADS = 16
QK_NOPE = 64
QK_ROPE = 32
V_HEAD = 64
ROPE_THETA = 10000.0
RMS_EPS = 1e-6
CONV_WIDTH = 3
SOFTMAX_SCALE = (QK_NOPE + QK_ROPE) ** -0.5
LOG2_E = 1.4426950408889634

LANES = 128
SUBLANES = 8
HALF_ROPE = QK_ROPE // 2
ROPE_LANE0 = QK_ROPE
VMEM_LIMIT = 56 * 1024 * 1024

ROW_TILE = 512
ATTN_TILE = 1024
ATTN_HEADS_PER_STEP = 4
SAMPLE_ATTN_CHUNKS = 4

bf16 = jnp.bfloat16
f32 = jnp.float32


def _rms(x, g):
    return x * lax.rsqrt(jnp.mean(x * x, axis=-1, keepdims=True) + RMS_EPS) * g


def _dot(a, b):
    return jnp.dot(a, b, preferred_element_type=f32)


def _dot_nt(a, b):
    return lax.dot_general(a, b, (((1,), (1,)), ((), ())), preferred_element_type=f32)


def _rope_block(blk, cos, sin):
    return blk * cos + pltpu.roll(blk, HALF_ROPE, 1) * sin


def _params(*sem):
    return pltpu.CompilerParams(dimension_semantics=sem, vmem_limit_bytes=VMEM_LIMIT)


def _const_spec(shape):
    nd = len(shape)
    return pl.BlockSpec(shape, lambda *_: (0,) * nd)


def _conv_layer_kernel(x_ref, init_ref, pre_g_ref, post_g_ref, w_in_ref, cw_ref, w_out_ref,
                       xo_ref, st_ref, vbuf, *, tile, off, shift, c_dim):
    i = pl.program_id(1)

    @pl.when(i == 0)
    def _():
        vbuf[0:off, :] = init_ref[0]

    x = x_ref[...]
    h = _rms(x, pre_g_ref[...]).astype(bf16)

    def proj(k):
        return _dot(h, w_in_ref[:, k * c_dim:(k + 1) * c_dim])

    vbuf[off:off + tile, :] = proj(1) * proj(2)
    cw = cw_ref[...]
    y = cw[0:1] * vbuf[off - 2 * shift:off - 2 * shift + tile, :]
    y = y + cw[1:2] * vbuf[off - shift:off - shift + tile, :]
    y = y + cw[2:3] * vbuf[off:off + tile, :]
    z = proj(3)
    g = (proj(0) * y * jax.nn.silu(z)).astype(bf16)
    m = _dot(g, w_out_ref[...])
    xo_ref[...] = x + _rms(m, post_g_ref[...])

    @pl.when(i == pl.num_programs(1) - 1)
    def _():
        st_ref[0] = vbuf[off + tile - 2 * shift:off + tile, :]

    vbuf[0:off, :] = vbuf[tile:tile + off, :]


def _conv_layer(x, init, pre_g, post_g, w_in, cw, w_out, *, nseq, tile, shift):
    rows, d = x.shape
    c_dim = cw.shape[1]
    t = rows // nseq
    nt = t // tile
    off = init.shape[1]
    ninit = init.shape[0]
    kern = functools.partial(_conv_layer_kernel, tile=tile, off=off, shift=shift, c_dim=c_dim)
    return pl.pallas_call(
        kern,
        grid=(nseq, nt),
        in_specs=[
            pl.BlockSpec((tile, d), lambda b, i: (b * nt + i, 0)),
            pl.BlockSpec((1, off, c_dim), (lambda b, i: (b, 0, 0)) if ninit > 1 else (lambda b, i: (0, 0, 0))),
            _const_spec((1, d)), _const_spec((1, d)),
            _const_spec(w_in.shape), _const_spec(cw.shape), _const_spec(w_out.shape),
        ],
        out_specs=[
            pl.BlockSpec((tile, d), lambda b, i: (b * nt + i, 0)),
            pl.BlockSpec((1, 2 * shift, c_dim), lambda b, i: (b, 0, 0)),
        ],
        out_shape=[jax.ShapeDtypeStruct((rows, d), f32),
                   jax.ShapeDtypeStruct((nseq, 2 * shift, c_dim), f32)],
        scratch_shapes=[pltpu.VMEM((off + tile, c_dim), f32)],
        compiler_params=_params("arbitrary", "arbitrary"),
        name="conv_layer",
    )(x, init, pre_g, post_g, w_in, cw, w_out)


def _latent_kernel(*refs, kv_lora, with_kv):
    if with_kv:
        (x_ref, g_ref, w_dkv_ref, lat_g_ref, cos_ref, sin_ref, w_uk_ref, w_uvt_ref,
         c_ref, kr_ref, k_ref, vt_ref) = refs
    else:
        x_ref, g_ref, w_dkv_ref, lat_g_ref, cos_ref, sin_ref, c_ref, kr_ref = refs
    xn = _rms(x_ref[...], g_ref[...]).astype(bf16)
    ckr = _dot(xn, w_dkv_ref[...])
    c = _rms(ckr[:, :kv_lora], lat_g_ref[...])
    krb = _rope_block(ckr[:, kv_lora:kv_lora + LANES], cos_ref[...], sin_ref[...])
    c_ref[...] = c
    kr_ref[...] = krb[:, ROPE_LANE0:ROPE_LANE0 + QK_ROPE]
    if with_kv:
        cb = c.astype(bf16)
        kn = _dot(cb, w_uk_ref[...])
        for h in range(N_HEADS):
            k_ref[:, h * LANES:(h + 1) * LANES] = (kn[:, h * LANES:(h + 1) * LANES] + krb).astype(bf16)
        pos = lax.broadcasted_iota(jnp.int32, (N_HEADS * LANES, 1), 0) % (2 * LANES)
        ones_rows = jnp.where(pos < V_HEAD, 0.0, jnp.where(pos < V_HEAD + LANES, 1.0, 0.0)).astype(f32)
        vt_ref[...] = (_dot_nt(w_uvt_ref[...], cb) + ones_rows).astype(bf16)


def _latent(x, g, w_dkv_pad, lat_g, tabs, w_uk_pad=None, w_uvt=None, *, t_out, tile, kv_tile=None):
    nseq, t_in, d = x.shape
    kv_lora = lat_g.shape[1]
    nt = t_in // tile
    with_kv = w_uk_pad is not None
    row_spec = lambda w: pl.BlockSpec((None, tile, w), lambda b, i: (b, i, 0))
    tab_spec = pl.BlockSpec((tile, LANES), lambda b, i: (i, 0))
    in_specs = [row_spec(d), _const_spec((1, d)), _const_spec(w_dkv_pad.shape), _const_spec((1, kv_lora)),
                tab_spec, tab_spec]
    args = [x, g, w_dkv_pad, lat_g, *tabs]
    out_specs = [row_spec(kv_lora), row_spec(QK_ROPE)]
    out_shape = [jax.ShapeDtypeStruct((nseq, t_out, kv_lora), f32), jax.ShapeDtypeStruct((nseq, t_out, QK_ROPE), f32)]
    if with_kv:
        per_kt = kv_tile // tile
        in_specs += [_const_spec(w_uk_pad.shape), _const_spec(w_uvt.shape)]
        args += [w_uk_pad, w_uvt]
        out_specs += [row_spec(N_HEADS * LANES),
                      pl.BlockSpec((None, None, N_HEADS * LANES, tile), lambda b, i: (b, i // per_kt, 0, i % per_kt))]
        out_shape += [jax.ShapeDtypeStruct((nseq, t_out, N_HEADS * LANES), bf16),
                      jax.ShapeDtypeStruct((nseq, -(-nt // per_kt), N_HEADS * LANES, kv_tile), bf16)]
    return pl.pallas_call(
        functools.partial(_latent_kernel, kv_lora=kv_lora, with_kv=with_kv),
        grid=(nseq, nt), in_specs=in_specs, out_specs=out_specs, out_shape=out_shape,
        compiler_params=_params("arbitrary", "arbitrary"),
        name="latent_kv" if with_kv else "latent",
    )(*args)


def _mla_query_kernel(*refs, q_lora, absorbed, rows):
    if absorbed:
        (x_ref, pre_g_ref, w_in_ref, qg_ref, w_uq_ref, cos_ref, sin_ref, w_ukt_ref,
         ql_ref, qr_ref, z_ref) = refs
    else:
        x_ref, pre_g_ref, w_in_ref, qg_ref, w_uq_ref, cos_ref, sin_ref, q_ref, z_ref = refs
    h = _rms(x_ref[...], pre_g_ref[...]).astype(bf16)
    q_lat = _dot(h, w_in_ref[:, :q_lora])
    z_ref[...] = _dot(h, w_in_ref[:, q_lora:]).astype(z_ref.dtype)
    qn = _rms(q_lat, qg_ref[...]).astype(bf16)
    q = _dot(qn, w_uq_ref[...])
    cos, sin = cos_ref[...], sin_ref[...]
    for hd in range(N_HEADS):
        blk = _rope_block(q[:, hd * LANES:(hd + 1) * LANES], cos, sin)
        if absorbed:
            ql = _dot(blk.astype(bf16), w_ukt_ref[hd])
            for c in range(ql.shape[1] // LANES):
                ql_ref[c, pl.ds(hd, rows, stride=N_HEADS), :] = ql[:, c * LANES:(c + 1) * LANES]
            qr_ref[pl.ds(hd, rows, stride=N_HEADS), :] = blk
        else:
            q_ref[:, hd * LANES:(hd + 1) * LANES] = blk.astype(bf16)


def _mla_query(x, pre_g, w_in, qg, w_uq_pad, tabs, w_ukt_pad=None, *, nseq, tile):
    rows, d = x.shape
    q_lora = qg.shape[1]
    z_dim = w_in.shape[1] - q_lora
    nt = rows // nseq // tile
    absorbed = w_ukt_pad is not None
    row_spec = lambda w: pl.BlockSpec((tile, w), lambda b, i: (b * nt + i, 0))
    tab_spec = pl.BlockSpec((tile, LANES), lambda b, i: (i, 0))
    in_specs = [row_spec(d), _const_spec((1, d)), _const_spec(w_in.shape), _const_spec((1, q_lora)),
                _const_spec(w_uq_pad.shape), tab_spec, tab_spec]
    args = [x, pre_g, w_in, qg, w_uq_pad, *tabs]
    if absorbed:
        assert nseq == 1 and nt == 1
        kv_lora = w_ukt_pad.shape[2]
        in_specs.append(_const_spec(w_ukt_pad.shape))
        args.append(w_ukt_pad)
        ql_shape = (kv_lora // LANES, rows * N_HEADS, LANES)
        out_specs = [_const_spec(ql_shape), _const_spec((rows * N_HEADS, LANES)), row_spec(z_dim)]
        out_shape = [jax.ShapeDtypeStruct(ql_shape, f32),
                     jax.ShapeDtypeStruct((rows * N_HEADS, LANES), f32),
                     jax.ShapeDtypeStruct((rows, z_dim), bf16)]
    else:
        out_specs = [row_spec(N_HEADS * LANES), row_spec(z_dim)]
        out_shape = [jax.ShapeDtypeStruct((rows, N_HEADS * LANES), bf16),
                     jax.ShapeDtypeStruct((rows, z_dim), bf16)]
    return pl.pallas_call(
        functools.partial(_mla_query_kernel, q_lora=q_lora, absorbed=absorbed, rows=rows),
        grid=(nseq, nt), in_specs=in_specs, out_specs=out_specs, out_shape=out_shape,
        compiler_params=_params("arbitrary", "arbitrary"),
        name="mla_query_absorbed" if absorbed else "mla_query",
    )(*args)


def _prompt_attn_kernel(q_ref, k_ref, vt_ref, o_ref, *, tile):
    i = pl.program_id(2)
    neg = jnp.finfo(f32).min
    heads = range(q_ref.shape[2] // LANES)
    n_sub = 2
    sub = tile // n_sub

    def keys(hh, start, size):
        return k_ref[0, pl.ds(start, size), hh * LANES:(hh + 1) * LANES]

    def values_t(hh, kt):
        return vt_ref[0, kt, hh * LANES:(hh + 1) * LANES, :]

    def col_max(x):
        return jnp.max(x, axis=0, keepdims=True)

    d0 = pl.multiple_of(i * tile, tile)
    grp = LANES

    def init_cols(hh, r):
        q = q_ref[0, r * sub:(r + 1) * sub, hh * LANES:(hh + 1) * LANES]
        nk = (r + 1) * sub
        key = lax.broadcasted_iota(jnp.int32, (nk, sub), 0)
        qry = lax.broadcasted_iota(jnp.int32, (nk, sub), 1) + (r * sub + N_META)
        st = jnp.where(key <= qry, _dot_nt(keys(hh, d0, nk), q), neg)
        m = col_max(st)
        acc = _dot(values_t(hh, i)[:, :nk], jnp.exp2(st - m).astype(bf16))
        sc = _dot_nt(keys(hh, d0 + nk, N_META), q[sub - grp:])
        keyc = lax.broadcasted_iota(jnp.int32, (N_META, grp), 0)
        qryc = lax.broadcasted_iota(jnp.int32, (N_META, grp), 1) - (grp - N_META)
        sc = jnp.where(keyc <= qryc, sc, neg)
        m_c = col_max(sc)
        vt_c = values_t(hh, i)[:, nk:nk + N_META] if nk < tile else values_t(hh, i + 1)[:, :N_META]
        acc_c = _dot(vt_c, jnp.exp2(sc - m_c).astype(bf16))
        m_tail = jnp.maximum(m[:, sub - grp:], m_c)
        acc_tail = jnp.exp2(m[:, sub - grp:] - m_tail) * acc[:, sub - grp:] + jnp.exp2(m_c - m_tail) * acc_c
        return (jnp.concatenate([m[:, :sub - grp], m_tail], axis=1),
                jnp.concatenate([acc[:, :sub - grp], acc_tail], axis=1))

    carry = []
    for hh in heads:
        parts = [init_cols(hh, r) for r in range(n_sub)]
        m_cat = jnp.concatenate([pt[0] for pt in parts], axis=1)
        carry += [col_max(jnp.broadcast_to(m_cat, (SUBLANES, tile))), jnp.concatenate([pt[1] for pt in parts], axis=1)]

    qs = [q_ref[0, :, hh * LANES:(hh + 1) * LANES] for hh in heads]

    def step(kt, carry):
        start = pl.multiple_of(kt * tile, tile)
        out = []
        for g in range(len(heads) // 2):
            pair = (2 * g, 2 * g + 1)
            sts = {hh: _dot_nt(keys(hh, start, tile), qs[hh]) for hh in pair}
            m_news = {hh: jnp.maximum(carry[2 * hh], col_max(sts[hh])) for hh in pair}
            ps = {hh: jnp.exp2(sts[hh] - m_news[hh]).astype(bf16) for hh in pair}
            for hh in pair:
                pv = _dot(values_t(hh, kt), ps[hh])
                out += [m_news[hh], jnp.exp2(carry[2 * hh] - m_news[hh]) * carry[2 * hh + 1] + pv]
        return tuple(out)

    carry = lax.fori_loop(0, i, step, tuple(carry))
    row = lax.broadcasted_iota(jnp.int32, (LANES, tile), 0)
    for g in range(len(heads) // 2):
        a_even, a_odd = carry[4 * g + 1], carry[4 * g + 3]
        num = jnp.where(row < V_HEAD, a_even, a_odd)
        den = jnp.where(row < V_HEAD, a_odd, a_even)
        den = jnp.concatenate([den[V_HEAD:], den[:V_HEAD]], axis=0)
        o_ref[0, :, g * LANES:(g + 1) * LANES] = jnp.transpose(num / den).astype(o_ref.dtype)


def _prompt_attn(q, k, vt, *, tile):
    b, t, _ = q.shape
    tk = k.shape[1]
    n_kt = vt.shape[1]
    nq = t // tile
    hps = ATTN_HEADS_PER_STEP
    return pl.pallas_call(
        functools.partial(_prompt_attn_kernel, tile=tile),
        grid=(b, N_HEADS // hps, nq),
        in_specs=[
            pl.BlockSpec((1, tile, hps * LANES), lambda b, g, i: (b, i, g)),
            pl.BlockSpec((1, tk, hps * LANES), lambda b, g, i: (b, 0, g)),
            pl.BlockSpec((1, n_kt, hps * LANES, tile), lambda b, g, i: (b, 0, g, 0)),
        ],
        out_specs=pl.BlockSpec((1, tile, hps * V_HEAD), lambda b, g, i: (b, i, g)),
        out_shape=jax.ShapeDtypeStruct((b, t, N_HEADS * V_HEAD), bf16),
        compiler_params=_params("arbitrary", "arbitrary", "arbitrary"),
        name="prompt_attn",
    )(q, k, vt)


def _sample_attn_kernel(pt_ref, ql_ref, qr_ref, cn_ref, krn_ref, ckv_hbm, krt_hbm, o_ref,
                        kv_land, kr_land, kb16, kr16, sem_kv, sem_kr, *, n_pages, page, n_chunks):
    b = pl.program_id(0)
    slot = lax.rem(b, 2)

    def page_copies(req, slot_, p):
        pg = pt_ref[req * n_pages + p]
        return (pltpu.make_async_copy(ckv_hbm.at[pg], kv_land.at[slot_, p], sem_kv.at[slot_]),
                pltpu.make_async_copy(krt_hbm.at[pg], kr_land.at[slot_, p], sem_kr.at[slot_]))

    def wait_slot(slot_):
        pltpu.make_async_copy(ckv_hbm.at[pl.ds(0, n_pages)], kv_land.at[slot_], sem_kv.at[slot_]).wait()
        pltpu.make_async_copy(krt_hbm.at[pl.ds(0, n_pages)], kr_land.at[slot_], sem_kr.at[slot_]).wait()

    @pl.when(b == 0)
    def _():
        def body(p, c):
            for cp in page_copies(0, 0, p):
                cp.start()
            return c
        lax.fori_loop(0, n_pages, body, 0)

    wait_slot(slot)
    nxt = jnp.minimum(b + 1, pl.num_programs(0) - 1)
    for p in range(n_pages):
        for cp in page_copies(nxt, 1 - slot, p):
            cp.start(priority=p % 2)

    n_half = ql_ref.shape[0]
    ql = jnp.concatenate([ql_ref[c] for c in range(n_half)], axis=-1).astype(bf16)
    qr = qr_ref[:, ROPE_LANE0:ROPE_LANE0 + QK_ROPE].astype(bf16)
    nq = ql.shape[0]

    ppc = n_pages // n_chunks
    parts = []
    for c in range(n_chunks):
        for p in range(c * ppc, (c + 1) * ppc):
            kb16[p * page:(p + 1) * page, :] = kv_land[slot, p].astype(bf16)
            kr16[:, p * page:(p + 1) * page] = kr_land[slot, p].astype(bf16)
        kb = kb16[c * ppc * page:(c + 1) * ppc * page, :]
        half = ppc * page // 2
        s = jnp.concatenate([_dot_nt(ql, kb[:half]), _dot_nt(ql, kb[half:])], axis=-1)
        s = s + _dot(qr, kr16[:, c * ppc * page:(c + 1) * ppc * page])
        m_c = jnp.max(s, axis=-1, keepdims=True)
        p_c = jnp.exp2(s - m_c)
        pb = p_c.astype(bf16)
        a_c = _dot(pb[:, :half], kb[:half]) + _dot(pb[:, half:], kb[half:])
        parts.append((m_c, jnp.sum(p_c, axis=-1, keepdims=True), a_c))
    cn = cn_ref[0].astype(bf16)
    krn = krn_ref[0].astype(bf16)
    npad = cn.shape[0]
    s_new = _dot_nt(ql, cn) + _dot_nt(qr, krn)
    row = lax.broadcasted_iota(jnp.int32, (nq, npad), 0)
    col = lax.broadcasted_iota(jnp.int32, (nq, npad), 1)
    s_new = jnp.where(col * N_HEADS <= row, s_new, jnp.finfo(f32).min)
    m_n = jnp.max(s_new, axis=-1, keepdims=True)
    p_n = jnp.exp2(s_new - m_n)
    parts.append((m_n, jnp.sum(p_n, axis=-1, keepdims=True), _dot(p_n.astype(bf16), cn)))

    m = functools.reduce(jnp.maximum, [pt_[0] for pt_ in parts])
    l = sum(jnp.exp2(m_c - m) * l_c for m_c, l_c, _ in parts)
    acc = sum(jnp.exp2(m_c - m) * a_c for m_c, _, a_c in parts)
    o = acc / l
    for c in range(n_half):
        o_ref[c] = o[:, c * LANES:(c + 1) * LANES]

    @pl.when(b == pl.num_programs(0) - 1)
    def _():
        wait_slot(1 - slot)


def _sample_attn(page_table, q_lat, q_rope, c_new, kr_new, cache_ckv, cache_krope_t):
    nreq, n_pages = page_table.shape
    _, page, kv_lora = cache_ckv.shape
    n_half = q_lat.shape[0]
    nq = q_lat.shape[1] // nreq
    lpad = c_new.shape[1]
    grid_spec = pltpu.PrefetchScalarGridSpec(
        num_scalar_prefetch=1, grid=(nreq,),
        in_specs=[
            pl.BlockSpec((n_half, nq, LANES), lambda b, pt: (0, b, 0)),
            pl.BlockSpec((nq, LANES), lambda b, pt: (b, 0)),
            pl.BlockSpec((1, lpad, kv_lora), lambda b, pt: (b, 0, 0)),
            pl.BlockSpec((1, lpad, QK_ROPE), lambda b, pt: (b, 0, 0)),
            pl.BlockSpec(memory_space=pl.ANY),
            pl.BlockSpec(memory_space=pl.ANY),
        ],
        out_specs=pl.BlockSpec((n_half, nq, LANES), lambda b, pt: (0, b, 0)),
        scratch_shapes=[
            pltpu.VMEM((2, n_pages, page, kv_lora), f32),
            pltpu.VMEM((2, n_pages, QK_ROPE, page), f32),
            pltpu.VMEM((n_pages * page, kv_lora), bf16),
            pltpu.VMEM((QK_ROPE, n_pages * page), bf16),
            pltpu.SemaphoreType.DMA((2,)),
            pltpu.SemaphoreType.DMA((2,)),
        ],
    )
    return pl.pallas_call(
        functools.partial(_sample_attn_kernel, n_pages=n_pages, page=page,
                          n_chunks=min(SAMPLE_ATTN_CHUNKS, n_pages)),
        grid_spec=grid_spec,
        out_shape=jax.ShapeDtypeStruct((n_half, nreq * nq, LANES), f32),
        compiler_params=_params("arbitrary"),
        name="sample_attn",
    )(page_table.reshape(-1), q_lat, q_rope, c_new, kr_new, cache_ckv, cache_krope_t)


def _mla_out_kernel(*refs, absorbed, rows):
    if absorbed:
        ol_ref, w_uv_ref, z_ref, x_ref, w_out_ref, post_g_ref, xo_ref = refs
        def head_rows(hd):
            halves = [ol_ref[c, pl.ds(hd, rows, stride=N_HEADS), :] for c in range(ol_ref.shape[0])]
            return jnp.concatenate(halves, axis=-1).astype(bf16)

        parts = []
        for g in range(N_HEADS // 2):
            parts.append(_dot(head_rows(2 * g), w_uv_ref[2 * g]) + _dot(head_rows(2 * g + 1), w_uv_ref[2 * g + 1]))
        o = jnp.concatenate(parts, axis=-1)
    else:
        o_ref, z_ref, x_ref, w_out_ref, post_g_ref, xo_ref = refs
        o = o_ref[...]
    g = (o.astype(f32) * jax.nn.silu(z_ref[...].astype(f32))).astype(bf16)
    m = _dot(g, w_out_ref[...])
    xo_ref[...] = x_ref[...] + _rms(m, post_g_ref[...])


def _mla_out(o, z, x, w_out, post_g, w_uv_pad=None, *, tile):
    rows, d = x.shape
    z_dim = z.shape[1]
    nt = rows // tile
    absorbed = w_uv_pad is not None
    row_spec = lambda w: pl.BlockSpec((tile, w), lambda i: (i, 0))
    if absorbed:
        assert nt == 1
        in_specs = [_const_spec(o.shape), _const_spec(w_uv_pad.shape)]
        args = [o, w_uv_pad]
    else:
        in_specs = [row_spec(z_dim)]
        args = [o]
    in_specs += [row_spec(z_dim), row_spec(d), _const_spec(w_out.shape), _const_spec((1, d))]
    args += [z, x, w_out, post_g]
    return pl.pallas_call(
        functools.partial(_mla_out_kernel, absorbed=absorbed, rows=rows),
        grid=(nt,), in_specs=in_specs, out_specs=row_spec(d),
        out_shape=jax.ShapeDtypeStruct((rows, d), f32),
        compiler_params=_params("arbitrary"),
        name="mla_out_absorbed" if absorbed else "mla_out",
    )(*args)


def _rope_tables(pos, scale):
    inv = ROPE_THETA ** (-jnp.arange(HALF_ROPE, dtype=f32) / HALF_ROPE)
    ang = pos.astype(f32)[:, None] * inv[None, :]
    cos, sin = jnp.cos(ang), jnp.sin(ang)
    r = pos.shape[0]
    zeros = lambda w: jnp.zeros((r, w), f32)
    c = jnp.concatenate([zeros(ROPE_LANE0), cos, cos, jnp.ones((r, QK_NOPE), f32)], axis=-1)
    s = jnp.concatenate([zeros(ROPE_LANE0), -sin, sin, zeros(QK_NOPE)], axis=-1)
    return c * scale, s * scale


def _head_blocks(w_rope, w_nope):
    blk = jnp.concatenate([w_rope, w_rope, w_nope], axis=-1)
    return blk.reshape(blk.shape[:-2] + (N_HEADS * LANES,))


def kernel(x_prompt, x_sample, cache_ckv, cache_krope, state_conv, page_table, meta_tokens,
           pre_norm_g, post_norm_g, w_in_conv, conv_w, w_out_conv, kv_norm_g, w_dkv,
           kv_lat_norm_g, w_uk, w_uv, w_in_mla, q_norm_g, w_uq, w_out_mla):
    bp, seq, d = x_prompt.shape
    bs, ls, _ = x_sample.shape
    n_a = w_in_conv.shape[0]
    n_b = w_in_mla.shape[0]
    c_dim = conv_w.shape[2]
    kv_lora = kv_lat_norm_g.shape[0]
    q_lora = q_norm_g.shape[1]
    past_len = page_table.shape[1] * cache_ckv.shape[1]

    row = lambda v: v.reshape(1, -1).astype(f32)
    w_in_conv_b = w_in_conv.astype(bf16)
    w_out_conv_b = w_out_conv.astype(bf16)
    w_in_mla_b = w_in_mla.astype(bf16)
    w_out_mla_b = w_out_mla.astype(bf16)
    w_dkv_pad = jnp.concatenate([w_dkv, w_dkv[:, kv_lora:], jnp.zeros((d, QK_NOPE), w_dkv.dtype)],
                                axis=-1).astype(bf16)
    uq = w_uq.reshape(n_b, q_lora, N_HEADS, QK_NOPE + QK_ROPE)
    w_uq_pad = _head_blocks(uq[..., QK_NOPE:], uq[..., :QK_NOPE]).astype(bf16)
    w_uk_blk = _head_blocks(jnp.zeros((kv_lora, N_HEADS, QK_ROPE), w_uk.dtype), w_uk)
    w_uk_pad = w_uk_blk.astype(bf16)
    w_ukt_pad = jnp.transpose(w_uk_blk.reshape(kv_lora, N_HEADS, LANES), (1, 2, 0)).astype(bf16)
    uv = jnp.transpose(w_uv, (1, 0, 2))
    zv = jnp.zeros_like(uv)
    even = (jnp.arange(N_HEADS) % 2 == 0)[:, None, None]
    w_uv_pad = jnp.concatenate([jnp.where(even, uv, zv), jnp.where(even, zv, uv)], axis=-1).astype(bf16)
    w_uvt_blk = jnp.transpose(w_uv_pad, (0, 2, 1)).reshape(N_HEADS * LANES, kv_lora)

    def trunk_a(x, inits, *, nseq, tile, shift):
        states = []
        for l in range(n_a):
            x, st = _conv_layer(x, inits[l], row(pre_norm_g[l]), row(post_norm_g[l]), w_in_conv_b[l],
                                conv_w[l], w_out_conv_b[l], nseq=nseq, tile=tile, shift=shift)
            states.append(st)
        return x, states

    zero_init = jnp.zeros((1, SUBLANES, c_dim), f32)
    xm, meta_states = trunk_a(meta_tokens.astype(f32), [zero_init] * n_a, nseq=1, tile=N_META, shift=1)

    tile_p = min(ROW_TILE, seq)
    inits_p = [jnp.concatenate([jnp.zeros((1, SUBLANES - 2, c_dim), f32), st], axis=1) for st in meta_states]
    xp, prompt_states = trunk_a(x_prompt.reshape(bp * seq, d), inits_p, nseq=bp, tile=tile_p, shift=1)
    t_pos = N_META + seq
    t_pad = -(-t_pos // tile_p) * tile_p
    x_pos = jnp.concatenate([jnp.broadcast_to(xm[None], (bp, N_META, d)), xp.reshape(bp, seq, d),
                             jnp.zeros((bp, t_pad - t_pos, d), f32)], axis=1)
    tabs_pos = _rope_tables(jnp.arange(t_pad, dtype=jnp.int32), 1.0)
    attn_tile = min(ATTN_TILE, seq)
    ckv_prompt, krope_prompt, k_p, vt_p = _latent(x_pos, row(kv_norm_g), w_dkv_pad, row(kv_lat_norm_g), tabs_pos,
                                                  w_uk_pad, w_uvt_blk, t_out=t_pos, tile=tile_p, kv_tile=attn_tile)
    tabs_q = _rope_tables(N_META + jnp.arange(seq, dtype=jnp.int32), SOFTMAX_SCALE * LOG2_E)
    for j in range(n_b):
        l = n_a + j
        q, z = _mla_query(xp, row(pre_norm_g[l]), w_in_mla_b[j], row(q_norm_g[j]), w_uq_pad[j], tabs_q,
                          nseq=bp, tile=tile_p)
        o = _prompt_attn(q.reshape(bp, seq, -1), k_p, vt_p, tile=attn_tile)
        xp = _mla_out(o.reshape(bp * seq, -1), z, xp, w_out_mla_b[j], row(post_norm_g[l]), tile=tile_p)
    y_prompt = xp.reshape(bp, seq, d)
    conv_prompt = jnp.stack([st for st in prompt_states])

    rs = bs * ls
    xs = jnp.transpose(x_sample, (1, 0, 2)).reshape(rs, d)
    inits_s = [jnp.transpose(state_conv[l], (1, 0, 2)).reshape(1, (CONV_WIDTH - 1) * bs, c_dim) for l in range(n_a)]
    xs, sample_states = trunk_a(xs, inits_s, nseq=1, tile=rs, shift=bs)
    conv_sample = jnp.stack([jnp.transpose(st.reshape(CONV_WIDTH - 1, bs, c_dim), (1, 0, 2)) for st in sample_states])
    xs = jnp.transpose(xs.reshape(ls, bs, d), (1, 0, 2)).reshape(rs, d)
    pos_s = jnp.tile(past_len + jnp.arange(ls, dtype=jnp.int32), bs)
    c_s, kr_s = _latent(xs[None], row(kv_norm_g), w_dkv_pad, row(kv_lat_norm_g), _rope_tables(pos_s, 1.0),
                        t_out=rs, tile=rs)
    lpad = 16
    c_new = jnp.pad(c_s.reshape(bs, ls, kv_lora), ((0, 0), (0, lpad - ls), (0, 0)))
    kr_new = jnp.pad(kr_s.reshape(bs, ls, QK_ROPE), ((0, 0), (0, lpad - ls), (0, 0)))
    tabs_qs = _rope_tables(pos_s, SOFTMAX_SCALE * LOG2_E)
    cache_krope_t = jnp.swapaxes(cache_krope, 1, 2)
    for j in range(n_b):
        l = n_a + j
        q_lat, q_rope, z = _mla_query(xs, row(pre_norm_g[l]), w_in_mla_b[j], row(q_norm_g[j]), w_uq_pad[j],
                                      tabs_qs, w_ukt_pad, nseq=1, tile=rs)
        o_lat = _sample_attn(page_table, q_lat, q_rope, c_new, kr_new, cache_ckv, cache_krope_t)
        xs = _mla_out(o_lat, z, xs, w_out_mla_b[j], row(post_norm_g[l]), w_uv_pad, tile=rs)
    y_sample = xs.reshape(bs, ls, d)
    ckv_sample = c_s.reshape(bs, ls, kv_lora)
    krope_sample = kr_s.reshape(bs, ls, QK_ROPE)

    return (y_prompt, y_sample, ckv_prompt, krope_prompt, conv_prompt, ckv_sample, krope_sample, conv_sample)
```

```python
import functools

import jax
import jax.numpy as jnp
from jax import lax
from jax.experimental import pallas as pl
from jax.experimental.pallas import tpu as pltpu

N_META = 16
N_HEADS = 16
QK_NOPE = 64
QK_ROPE = 32
V_HEAD = 64
ROPE_THETA = 10000.0
RMS_EPS = 1e-6
CONV_WIDTH = 3
SOFTMAX_SCALE = (QK_NOPE + QK_ROPE) ** -0.5
LOG2_E = 1.4426950408889634

LANES = 128
SUBLANES = 8
HALF_ROPE = QK_ROPE // 2
ROPE_LANE0 = QK_ROPE
VMEM_LIMIT = 56 * 1024 * 1024

ROW_TILE = 512
ATTN_TILE = 1024
ATTN_HEADS_PER_STEP = 4
ATTN_DIAG_SPLITS = 2
SAMPLE_ATTN_CHUNKS = 4

bf16 = jnp.bfloat16
f32 = jnp.float32


def _rms(x, g):
    return x * lax.rsqrt(jnp.mean(x * x, axis=-1, keepdims=True) + RMS_EPS) * g


def _dot(a, b):
    return jnp.dot(a, b, preferred_element_type=f32)


def _dot_nt(a, b):
    return lax.dot_general(a, b, (((1,), (1,)), ((), ())), preferred_element_type=f32)


def _rope_block(blk, cos, sin):
    return blk * cos + pltpu.roll(blk, HALF_ROPE, 1) * sin


def _params(*sem):
    return pltpu.CompilerParams(dimension_semantics=sem, vmem_limit_bytes=VMEM_LIMIT)


def _const_spec(shape):
    nd = len(shape)
    return pl.BlockSpec(shape, lambda *_: (0,) * nd)


def _conv_layer_kernel(x_ref, init_ref, pre_g_ref, post_g_ref, w_in_ref, cw_ref, w_out_ref,
                       xo_ref, st_ref, vbuf, *, tile, off, shift, c_dim):
    i = pl.program_id(1)

    @pl.when(i == 0)
    def _():
        vbuf[0:off, :] = init_ref[0]

    x = x_ref[...]
    h = _rms(x, pre_g_ref[...]).astype(bf16)

    def proj(k):
        return _dot(h, w_in_ref[:, k * c_dim:(k + 1) * c_dim])

    vbuf[off:off + tile, :] = proj(1) * proj(2)
    cw = cw_ref[...]
    y = cw[0:1] * vbuf[off - 2 * shift:off - 2 * shift + tile, :]
    y = y + cw[1:2] * vbuf[off - shift:off - shift + tile, :]
    y = y + cw[2:3] * vbuf[off:off + tile, :]
    z = proj(3)
    g = (proj(0) * y * jax.nn.silu(z)).astype(bf16)
    m = _dot(g, w_out_ref[...])
    xo_ref[...] = x + _rms(m, post_g_ref[...])

    @pl.when(i == pl.num_programs(1) - 1)
    def _():
        st_ref[0] = vbuf[off + tile - 2 * shift:off + tile, :]

    vbuf[0:off, :] = vbuf[tile:tile + off, :]


def _conv_layer(x, init, pre_g, post_g, w_in, cw, w_out, *, nseq, tile, shift):
    rows, d = x.shape
    c_dim = cw.shape[1]
    t = rows // nseq
    nt = t // tile
    off = init.shape[1]
    ninit = init.shape[0]
    kern = functools.partial(_conv_layer_kernel, tile=tile, off=off, shift=shift, c_dim=c_dim)
    return pl.pallas_call(
        kern,
        grid=(nseq, nt),
        in_specs=[
            pl.BlockSpec((tile, d), lambda b, i: (b * nt + i, 0)),
            pl.BlockSpec((1, off, c_dim), (lambda b, i: (b, 0, 0)) if ninit > 1 else (lambda b, i: (0, 0, 0))),
            _const_spec((1, d)), _const_spec((1, d)),
            _const_spec(w_in.shape), _const_spec(cw.shape), _const_spec(w_out.shape),
        ],
        out_specs=[
            pl.BlockSpec((tile, d), lambda b, i: (b * nt + i, 0)),
            pl.BlockSpec((1, 2 * shift, c_dim), lambda b, i: (b, 0, 0)),
        ],
        out_shape=[jax.ShapeDtypeStruct((rows, d), f32),
                   jax.ShapeDtypeStruct((nseq, 2 * shift, c_dim), f32)],
        scratch_shapes=[pltpu.VMEM((off + tile, c_dim), f32)],
        compiler_params=_params("arbitrary", "arbitrary"),
        name="conv_layer",
    )(x, init, pre_g, post_g, w_in, cw, w_out)


def _latent_kernel(*refs, kv_lora, with_kv):
    if with_kv:
        (x_ref, g_ref, w_dkv_ref, lat_g_ref, cos_ref, sin_ref, w_uk_ref, w_uv_ref,
         c_ref, kr_ref, k_ref, v_ref) = refs
    else:
        x_ref, g_ref, w_dkv_ref, lat_g_ref, cos_ref, sin_ref, c_ref, kr_ref = refs
    xn = _rms(x_ref[...], g_ref[...]).astype(bf16)
    ckr = _dot(xn, w_dkv_ref[...])
    c = _rms(ckr[:, :kv_lora], lat_g_ref[...])
    krb = _rope_block(ckr[:, kv_lora:kv_lora + LANES], cos_ref[...], sin_ref[...])
    c_ref[...] = c
    kr_ref[...] = krb[:, ROPE_LANE0:ROPE_LANE0 + QK_ROPE]
    if with_kv:
        cb = c.astype(bf16)
        kn = _dot(cb, w_uk_ref[...])
        for h in range(N_HEADS):
            k_ref[:, h * LANES:(h + 1) * LANES] = (kn[:, h * LANES:(h + 1) * LANES] + krb).astype(bf16)
        pos = lax.broadcasted_iota(jnp.int32, (1, N_HEADS * LANES), 1) % (2 * LANES)
        ones_lanes = jnp.where(pos < V_HEAD, 0.0, jnp.where(pos < V_HEAD + LANES, 1.0, 0.0)).astype(f32)
        v_ref[...] = (_dot(cb, w_uv_ref[...]) + ones_lanes).astype(bf16)


def _latent(x, g, w_dkv_pad, lat_g, tabs, w_uk_pad=None, w_uv=None, *, t_out, tile):
    nseq, t_in, d = x.shape
    kv_lora = lat_g.shape[1]
    nt = t_in // tile
    with_kv = w_uk_pad is not None
    row_spec = lambda w: pl.BlockSpec((None, tile, w), lambda b, i: (b, i, 0))
    tab_spec = pl.BlockSpec((tile, LANES), lambda b, i: (i, 0))
    in_specs = [row_spec(d), _const_spec((1, d)), _const_spec(w_dkv_pad.shape), _const_spec((1, kv_lora)),
                tab_spec, tab_spec]
    args = [x, g, w_dkv_pad, lat_g, *tabs]
    out_specs = [row_spec(kv_lora), row_spec(QK_ROPE)]
    out_shape = [jax.ShapeDtypeStruct((nseq, t_out, kv_lora), f32), jax.ShapeDtypeStruct((nseq, t_out, QK_ROPE), f32)]
    if with_kv:
        in_specs += [_const_spec(w_uk_pad.shape), _const_spec(w_uv.shape)]
        args += [w_uk_pad, w_uv]
        out_specs += [row_spec(N_HEADS * LANES), row_spec(N_HEADS * LANES)]
        out_shape += [jax.ShapeDtypeStruct((nseq, t_out, N_HEADS * LANES), bf16),
                      jax.ShapeDtypeStruct((nseq, t_out, N_HEADS * LANES), bf16)]
    return pl.pallas_call(
        functools.partial(_latent_kernel, kv_lora=kv_lora, with_kv=with_kv),
        grid=(nseq, nt), in_specs=in_specs, out_specs=out_specs, out_shape=out_shape,
        compiler_params=_params("arbitrary", "arbitrary"),
        name="latent_kv" if with_kv else "latent",
    )(*args)


def _mla_query_kernel(*refs, q_lora, absorbed, rows):
    if absorbed:
        (x_ref, pre_g_ref, w_in_ref, qg_ref, w_uq_ref, cos_ref, sin_ref, w_ukt_ref,
         ql_ref, qr_ref, z_ref) = refs
    else:
        x_ref, pre_g_ref, w_in_ref, qg_ref, w_uq_ref, cos_ref, sin_ref, q_ref, z_ref = refs
    h = _rms(x_ref[...], pre_g_ref[...]).astype(bf16)
    q_lat = _dot(h, w_in_ref[:, :q_lora])
    z_ref[...] = _dot(h, w_in_ref[:, q_lora:]).astype(z_ref.dtype)
    qn = _rms(q_lat, qg_ref[...]).astype(bf16)
    q = _dot(qn, w_uq_ref[...])
    cos, sin = cos_ref[...], sin_ref[...]
    for hd in range(N_HEADS):
        blk = _rope_block(q[:, hd * LANES:(hd + 1) * LANES], cos, sin)
        if absorbed:
            ql = _dot(blk.astype(bf16), w_ukt_ref[hd])
            for c in range(ql.shape[1] // LANES):
                ql_ref[c, pl.ds(hd, rows, stride=N_HEADS), :] = ql[:, c * LANES:(c + 1) * LANES]
            qr_ref[pl.ds(hd, rows, stride=N_HEADS), :] = blk
        else:
            q_ref[:, hd * LANES:(hd + 1) * LANES] = blk.astype(bf16)


def _mla_query(x, pre_g, w_in, qg, w_uq_pad, tabs, w_ukt_pad=None, *, nseq, tile):
    rows, d = x.shape
    q_lora = qg.shape[1]
    z_dim = w_in.shape[1] - q_lora
    nt = rows // nseq // tile
    absorbed = w_ukt_pad is not None
    row_spec = lambda w: pl.BlockSpec((tile, w), lambda b, i: (b * nt + i, 0))
    tab_spec = pl.BlockSpec((tile, LANES), lambda b, i: (i, 0))
    in_specs = [row_spec(d), _const_spec((1, d)), _const_spec(w_in.shape), _const_spec((1, q_lora)),
                _const_spec(w_uq_pad.shape), tab_spec, tab_spec]
    args = [x, pre_g, w_in, qg, w_uq_pad, *tabs]
    if absorbed:
        assert nseq == 1 and nt == 1
        kv_lora = w_ukt_pad.shape[2]
        in_specs.append(_const_spec(w_ukt_pad.shape))
        args.append(w_ukt_pad)
        ql_shape = (kv_lora // LANES, rows * N_HEADS, LANES)
        out_specs = [_const_spec(ql_shape), _const_spec((rows * N_HEADS, LANES)), row_spec(z_dim)]
        out_shape = [jax.ShapeDtypeStruct(ql_shape, f32),
                     jax.ShapeDtypeStruct((rows * N_HEADS, LANES), f32),
                     jax.ShapeDtypeStruct((rows, z_dim), bf16)]
    else:
        out_specs = [row_spec(N_HEADS * LANES), row_spec(z_dim)]
        out_shape = [jax.ShapeDtypeStruct((rows, N_HEADS * LANES), bf16),
                     jax.ShapeDtypeStruct((rows, z_dim), bf16)]
    return pl.pallas_call(
        functools.partial(_mla_query_kernel, q_lora=q_lora, absorbed=absorbed, rows=rows),
        grid=(nseq, nt), in_specs=in_specs, out_specs=out_specs, out_shape=out_shape,
        compiler_params=_params("arbitrary", "arbitrary"),
        name="mla_query_absorbed" if absorbed else "mla_query",
    )(*args)


def _prompt_attn_kernel(q_ref, k_ref, v_ref, o_ref, *, tile):
    i = pl.program_id(2)
    neg = jnp.finfo(f32).min
    heads = range(q_ref.shape[2] // LANES)
    n_sub = ATTN_DIAG_SPLITS
    sub = tile // n_sub

    def keys(hh, start, size):
        return k_ref[0, pl.ds(start, size), hh * LANES:(hh + 1) * LANES]

    def values(hh, start, size):
        return v_ref[0, pl.ds(start, size), hh * LANES:(hh + 1) * LANES]

    d0 = pl.multiple_of(i * tile, tile)
    head_rows = sub - N_META

    def init_rows(hh, r):
        q = q_ref[0, r * sub:(r + 1) * sub, hh * LANES:(hh + 1) * LANES]
        nk = (r + 1) * sub
        row = lax.broadcasted_iota(jnp.int32, (sub, nk), 0) + (r * sub + N_META)
        col = lax.broadcasted_iota(jnp.int32, (sub, nk), 1)
        s = jnp.where(col <= row, _dot_nt(q, keys(hh, d0, nk)), neg)
        m = jnp.max(s, axis=-1, keepdims=True)
        acc = _dot(jnp.exp2(s - m).astype(bf16), values(hh, d0, nk))
        sc = _dot_nt(q[head_rows:], keys(hh, d0 + nk, N_META))
        rowc = lax.broadcasted_iota(jnp.int32, (N_META, N_META), 0)
        colc = lax.broadcasted_iota(jnp.int32, (N_META, N_META), 1)
        sc = jnp.where(colc <= rowc, sc, neg)
        m_c = jnp.max(sc, axis=-1, keepdims=True)
        acc_c = _dot(jnp.exp2(sc - m_c).astype(bf16), values(hh, d0 + nk, N_META))
        m_tail = jnp.maximum(m[head_rows:], m_c)
        acc_tail = jnp.exp2(m[head_rows:] - m_tail) * acc[head_rows:] + jnp.exp2(m_c - m_tail) * acc_c
        m_b = jnp.concatenate([jnp.broadcast_to(m[:head_rows], (head_rows, LANES)),
                               jnp.broadcast_to(m_tail, (N_META, LANES))], axis=0)
        return m_b, jnp.concatenate([acc[:head_rows], acc_tail], axis=0)

    carry = []
    for hh in heads:
        parts = [init_rows(hh, r) for r in range(n_sub)]
        m_rows = jnp.concatenate([pt[0] for pt in parts], axis=0)
        carry += [jnp.max(m_rows, axis=-1, keepdims=True), jnp.concatenate([pt[1] for pt in parts], axis=0)]

    qs = [q_ref[0, :, hh * LANES:(hh + 1) * LANES] for hh in heads]

    def step(carry, start, size):
        out = []
        for hh in heads:
            m, acc = carry[2 * hh], carry[2 * hh + 1]
            s = _dot_nt(qs[hh], keys(hh, start, size))
            m_new = jnp.maximum(m, jnp.max(s, axis=-1, keepdims=True))
            alpha = jnp.exp2(m - m_new)
            p = jnp.exp2(s - m_new)
            out += [m_new, alpha * acc + _dot(p.astype(bf16), values(hh, start, size))]
        return tuple(out)

    carry = lax.fori_loop(0, i, lambda j, c: step(c, pl.multiple_of(j * tile, tile), tile), tuple(carry))
    lane = lax.broadcasted_iota(jnp.int32, (tile, LANES), 1)
    for g in range(len(heads) // 2):
        a_even, a_odd = carry[4 * g + 1], carry[4 * g + 3]
        num = jnp.where(lane < V_HEAD, a_even, a_odd)
        den = pltpu.roll(jnp.where(lane < V_HEAD, a_odd, a_even), V_HEAD, 1)
        o_ref[0, :, g * LANES:(g + 1) * LANES] = (num / den).astype(o_ref.dtype)


def _prompt_attn(q, k, v, *, tile):
    b, t, _ = q.shape
    tk = k.shape[1]
    nq = t // tile
    hps = ATTN_HEADS_PER_STEP
    return pl.pallas_call(
        functools.partial(_prompt_attn_kernel, tile=tile),
        grid=(b, N_HEADS // hps, nq),
        in_specs=[
            pl.BlockSpec((1, tile, hps * LANES), lambda b, g, i: (b, i, g)),
            pl.BlockSpec((1, tk, hps * LANES), lambda b, g, i: (b, 0, g)),
            pl.BlockSpec((1, tk, hps * LANES), lambda b, g, i: (b, 0, g)),
        ],
        out_specs=pl.BlockSpec((1, tile, hps * V_HEAD), lambda b, g, i: (b, i, g)),
        out_shape=jax.ShapeDtypeStruct((b, t, N_HEADS * V_HEAD), bf16),
        compiler_params=_params("arbitrary", "arbitrary", "arbitrary"),
        name="prompt_attn",
    )(q, k, v)


def _sample_attn_kernel(pt_ref, ql_ref, qr_ref, cn_ref, krn_ref, ckv_hbm, krt_hbm, o_ref,
                        kv_land, kr_land, kb16, kr16, sem_kv, sem_kr, *, n_pages, page, n_chunks):
    b = pl.program_id(0)
    slot = lax.rem(b, 2)

    def page_copies(req, slot_, p):
        pg = pt_ref[req * n_pages + p]
        return (pltpu.make_async_copy(ckv_hbm.at[pg], kv_land.at[slot_, p], sem_kv.at[slot_]),
                pltpu.make_async_copy(krt_hbm.at[pg], kr_land.at[slot_, p], sem_kr.at[slot_]))

    def wait_slot(slot_):
        pltpu.make_async_copy(ckv_hbm.at[pl.ds(0, n_pages)], kv_land.at[slot_], sem_kv.at[slot_]).wait()
        pltpu.make_async_copy(krt_hbm.at[pl.ds(0, n_pages)], kr_land.at[slot_], sem_kr.at[slot_]).wait()

    @pl.when(b == 0)
    def _():
        def body(p, c):
            for cp in page_copies(0, 0, p):
                cp.start()
            return c
        lax.fori_loop(0, n_pages, body, 0)

    wait_slot(slot)
    nxt = jnp.minimum(b + 1, pl.num_programs(0) - 1)
    for p in range(n_pages):
        for cp in page_copies(nxt, 1 - slot, p):
            cp.start(priority=p % 2)

    n_half = ql_ref.shape[0]
    ql = jnp.concatenate([ql_ref[c] for c in range(n_half)], axis=-1).astype(bf16)
    qr = qr_ref[:, ROPE_LANE0:ROPE_LANE0 + QK_ROPE].astype(bf16)
    nq = ql.shape[0]

    ppc = n_pages // n_chunks
    parts = []
    for c in range(n_chunks):
        for p in range(c * ppc, (c + 1) * ppc):
            kb16[p * page:(p + 1) * page, :] = kv_land[slot, p].astype(bf16)
            kr16[:, p * page:(p + 1) * page] = kr_land[slot, p].astype(bf16)
        kb = kb16[c * ppc * page:(c + 1) * ppc * page, :]
        half = ppc * page // 2
        s = jnp.concatenate([_dot_nt(ql, kb[:half]), _dot_nt(ql, kb[half:])], axis=-1)
        s = s + _dot(qr, kr16[:, c * ppc * page:(c + 1) * ppc * page])
        m_c = jnp.max(s, axis=-1, keepdims=True)
        p_c = jnp.exp2(s - m_c)
        pb = p_c.astype(bf16)
        a_c = _dot(pb[:, :half], kb[:half]) + _dot(pb[:, half:], kb[half:])
        parts.append((m_c, jnp.sum(p_c, axis=-1, keepdims=True), a_c))
    cn = cn_ref[0].astype(bf16)
    krn = krn_ref[0].astype(bf16)
    npad = cn.shape[0]
    s_new = _dot_nt(ql, cn) + _dot_nt(qr, krn)
    row = lax.broadcasted_iota(jnp.int32, (nq, npad), 0)
    col = lax.broadcasted_iota(jnp.int32, (nq, npad), 1)
    s_new = jnp.where(col * N_HEADS <= row, s_new, jnp.finfo(f32).min)
    m_n = jnp.max(s_new, axis=-1, keepdims=True)
    p_n = jnp.exp2(s_new - m_n)
    parts.append((m_n, jnp.sum(p_n, axis=-1, keepdims=True), _dot(p_n.astype(bf16), cn)))

    m = functools.reduce(jnp.maximum, [pt_[0] for pt_ in parts])
    l = sum(jnp.exp2(m_c - m) * l_c for m_c, l_c, _ in parts)
    acc = sum(jnp.exp2(m_c - m) * a_c for m_c, _, a_c in parts)
    o = acc / l
    for c in range(n_half):
        o_ref[c] = o[:, c * LANES:(c + 1) * LANES]

    @pl.when(b == pl.num_programs(0) - 1)
    def _():
        wait_slot(1 - slot)


def _sample_attn(page_table, q_lat, q_rope, c_new, kr_new, cache_ckv, cache_krope_t):
    nreq, n_pages = page_table.shape
    _, page, kv_lora = cache_ckv.shape
    n_half = q_lat.shape[0]
    nq = q_lat.shape[1] // nreq
    lpad = c_new.shape[1]
    grid_spec = pltpu.PrefetchScalarGridSpec(
        num_scalar_prefetch=1, grid=(nreq,),
        in_specs=[
            pl.BlockSpec((n_half, nq, LANES), lambda b, pt: (0, b, 0)),
            pl.BlockSpec((nq, LANES), lambda b, pt: (b, 0)),
            pl.BlockSpec((1, lpad, kv_lora), lambda b, pt: (b, 0, 0)),
            pl.BlockSpec((1, lpad, QK_ROPE), lambda b, pt: (b, 0, 0)),
            pl.BlockSpec(memory_space=pl.ANY),
            pl.BlockSpec(memory_space=pl.ANY),
        ],
        out_specs=pl.BlockSpec((n_half, nq, LANES), lambda b, pt: (0, b, 0)),
        scratch_shapes=[
            pltpu.VMEM((2, n_pages, page, kv_lora), f32),
            pltpu.VMEM((2, n_pages, QK_ROPE, page), f32),
            pltpu.VMEM((n_pages * page, kv_lora), bf16),
            pltpu.VMEM((QK_ROPE, n_pages * page), bf16),
            pltpu.SemaphoreType.DMA((2,)),
            pltpu.SemaphoreType.DMA((2,)),
        ],
    )
    return pl.pallas_call(
        functools.partial(_sample_attn_kernel, n_pages=n_pages, page=page,
                          n_chunks=min(SAMPLE_ATTN_CHUNKS, n_pages)),
        grid_spec=grid_spec,
        out_shape=jax.ShapeDtypeStruct((n_half, nreq * nq, LANES), f32),
        compiler_params=_params("arbitrary"),
        name="sample_attn",
    )(page_table.reshape(-1), q_lat, q_rope, c_new, kr_new, cache_ckv, cache_krope_t)


def _mla_out_kernel(*refs, absorbed, rows):
    if absorbed:
        ol_ref, w_uv_ref, z_ref, x_ref, w_out_ref, post_g_ref, xo_ref = refs
        def head_rows(hd):
            halves = [ol_ref[c, pl.ds(hd, rows, stride=N_HEADS), :] for c in range(ol_ref.shape[0])]
            return jnp.concatenate(halves, axis=-1).astype(bf16)

        parts = []
        for g in range(N_HEADS // 2):
            parts.append(_dot(head_rows(2 * g), w_uv_ref[2 * g]) + _dot(head_rows(2 * g + 1), w_uv_ref[2 * g + 1]))
        o = jnp.concatenate(parts, axis=-1)
    else:
        o_ref, z_ref, x_ref, w_out_ref, post_g_ref, xo_ref = refs
        o = o_ref[...]
    g = (o.astype(f32) * jax.nn.silu(z_ref[...].astype(f32))).astype(bf16)
    m = _dot(g, w_out_ref[...])
    xo_ref[...] = x_ref[...] + _rms(m, post_g_ref[...])


def _mla_out(o, z, x, w_out, post_g, w_uv_pad=None, *, tile):
    rows, d = x.shape
    z_dim = z.shape[1]
    nt = rows // tile
    absorbed = w_uv_pad is not None
    row_spec = lambda w: pl.BlockSpec((tile, w), lambda i: (i, 0))
    if absorbed:
        assert nt == 1
        in_specs = [_const_spec(o.shape), _const_spec(w_uv_pad.shape)]
        args = [o, w_uv_pad]
    else:
        in_specs = [row_spec(z_dim)]
        args = [o]
    in_specs += [row_spec(z_dim), row_spec(d), _const_spec(w_out.shape), _const_spec((1, d))]
    args += [z, x, w_out, post_g]
    return pl.pallas_call(
        functools.partial(_mla_out_kernel, absorbed=absorbed, rows=rows),
        grid=(nt,), in_specs=in_specs, out_specs=row_spec(d),
        out_shape=jax.ShapeDtypeStruct((rows, d), f32),
        compiler_params=_params("arbitrary"),
        name="mla_out_absorbed" if absorbed else "mla_out",
    )(*args)


def _rope_tables(pos, scale):
    inv = ROPE_THETA ** (-jnp.arange(HALF_ROPE, dtype=f32) / HALF_ROPE)
    ang = pos.astype(f32)[:, None] * inv[None, :]
    cos, sin = jnp.cos(ang), jnp.sin(ang)
    r = pos.shape[0]
    zeros = lambda w: jnp.zeros((r, w), f32)
    c = jnp.concatenate([zeros(ROPE_LANE0), cos, cos, jnp.ones((r, QK_NOPE), f32)], axis=-1)
    s = jnp.concatenate([zeros(ROPE_LANE0), -sin, sin, zeros(QK_NOPE)], axis=-1)
    return c * scale, s * scale


def _head_blocks(w_rope, w_nope):
    blk = jnp.concatenate([w_rope, w_rope, w_nope], axis=-1)
    return blk.reshape(blk.shape[:-2] + (N_HEADS * LANES,))


def kernel(x_prompt, x_sample, cache_ckv, cache_krope, state_conv, page_table, meta_tokens,
           pre_norm_g, post_norm_g, w_in_conv, conv_w, w_out_conv, kv_norm_g, w_dkv,
           kv_lat_norm_g, w_uk, w_uv, w_in_mla, q_norm_g, w_uq, w_out_mla):
    bp, seq, d = x_prompt.shape
    bs, ls, _ = x_sample.shape
    n_a = w_in_conv.shape[0]
    n_b = w_in_mla.shape[0]
    c_dim = conv_w.shape[2]
    kv_lora = kv_lat_norm_g.shape[0]
    q_lora = q_norm_g.shape[1]
    past_len = page_table.shape[1] * cache_ckv.shape[1]

    row = lambda v: v.reshape(1, -1).astype(f32)
    w_in_conv_b = w_in_conv.astype(bf16)
    w_out_conv_b = w_out_conv.astype(bf16)
    w_in_mla_b = w_in_mla.astype(bf16)
    w_out_mla_b = w_out_mla.astype(bf16)
    w_dkv_pad = jnp.concatenate([w_dkv, w_dkv[:, kv_lora:], jnp.zeros((d, QK_NOPE), w_dkv.dtype)],
                                axis=-1).astype(bf16)
    uq = w_uq.reshape(n_b, q_lora, N_HEADS, QK_NOPE + QK_ROPE)
    w_uq_pad = _head_blocks(uq[..., QK_NOPE:], uq[..., :QK_NOPE]).astype(bf16)
    w_uk_blk = _head_blocks(jnp.zeros((kv_lora, N_HEADS, QK_ROPE), w_uk.dtype), w_uk)
    w_uk_pad = w_uk_blk.astype(bf16)
    w_ukt_pad = jnp.transpose(w_uk_blk.reshape(kv_lora, N_HEADS, LANES), (1, 2, 0)).astype(bf16)
    uv = jnp.transpose(w_uv, (1, 0, 2))
    zv = jnp.zeros_like(uv)
    even = (jnp.arange(N_HEADS) % 2 == 0)[:, None, None]
    w_uv_pad = jnp.concatenate([jnp.where(even, uv, zv), jnp.where(even, zv, uv)], axis=-1).astype(bf16)
    w_uv_blk = jnp.transpose(w_uv_pad, (1, 0, 2)).reshape(kv_lora, N_HEADS * LANES)

    def trunk_a(x, inits, *, nseq, tile, shift):
        states = []
        for l in range(n_a):
            x, st = _conv_layer(x, inits[l], row(pre_norm_g[l]), row(post_norm_g[l]), w_in_conv_b[l],
                                conv_w[l], w_out_conv_b[l], nseq=nseq, tile=tile, shift=shift)
            states.append(st)
        return x, states

    zero_init = jnp.zeros((1, SUBLANES, c_dim), f32)
    xm, meta_states = trunk_a(meta_tokens.astype(f32), [zero_init] * n_a, nseq=1, tile=N_META, shift=1)

    tile_p = min(ROW_TILE, seq)
    inits_p = [jnp.concatenate([jnp.zeros((1, SUBLANES - 2, c_dim), f32), st], axis=1) for st in meta_states]
    xp, prompt_states = trunk_a(x_prompt.reshape(bp * seq, d), inits_p, nseq=bp, tile=tile_p, shift=1)
    t_pos = N_META + seq
    t_pad = -(-t_pos // tile_p) * tile_p
    x_pos = jnp.concatenate([jnp.broadcast_to(xm[None], (bp, N_META, d)), xp.reshape(bp, seq, d),
                             jnp.zeros((bp, t_pad - t_pos, d), f32)], axis=1)
    tabs_pos = _rope_tables(jnp.arange(t_pad, dtype=jnp.int32), 1.0)
    ckv_prompt, krope_prompt, k_p, v_p = _latent(x_pos, row(kv_norm_g), w_dkv_pad, row(kv_lat_norm_g), tabs_pos,
                                                 w_uk_pad, w_uv_blk, t_out=t_pos, tile=tile_p)
    tabs_q = _rope_tables(N_META + jnp.arange(seq, dtype=jnp.int32), SOFTMAX_SCALE * LOG2_E)
    attn_tile = min(ATTN_TILE, seq)
    for j in range(n_b):
        l = n_a + j
        q, z = _mla_query(xp, row(pre_norm_g[l]), w_in_mla_b[j], row(q_norm_g[j]), w_uq_pad[j], tabs_q,
                          nseq=bp, tile=tile_p)
        o = _prompt_attn(q.reshape(bp, seq, -1), k_p, v_p, tile=attn_tile)
        xp = _mla_out(o.reshape(bp * seq, -1), z, xp, w_out_mla_b[j], row(post_norm_g[l]), tile=tile_p)
    y_prompt = xp.reshape(bp, seq, d)
    conv_prompt = jnp.stack([st for st in prompt_states])

    rs = bs * ls
    xs = jnp.transpose(x_sample, (1, 0, 2)).reshape(rs, d)
    inits_s = [jnp.transpose(state_conv[l], (1, 0, 2)).reshape(1, (CONV_WIDTH - 1) * bs, c_dim) for l in range(n_a)]
    xs, sample_states = trunk_a(xs, inits_s, nseq=1, tile=rs, shift=bs)
    conv_sample = jnp.stack([jnp.transpose(st.reshape(CONV_WIDTH - 1, bs, c_dim), (1, 0, 2)) for st in sample_states])
    xs = jnp.transpose(xs.reshape(ls, bs, d), (1, 0, 2)).reshape(rs, d)
    pos_s = jnp.tile(past_len + jnp.arange(ls, dtype=jnp.int32), bs)
    c_s, kr_s = _latent(xs[None], row(kv_norm_g), w_dkv_pad, row(kv_lat_norm_g), _rope_tables(pos_s, 1.0),
                        t_out=rs, tile=rs)
    lpad = 16
    c_new = jnp.pad(c_s.reshape(bs, ls, kv_lora), ((0, 0), (0, lpad - ls), (0, 0)))
    kr_new = jnp.pad(kr_s.reshape(bs, ls, QK_ROPE), ((0, 0), (0, lpad - ls), (0, 0)))
    tabs_qs = _rope_tables(pos_s, SOFTMAX_SCALE * LOG2_E)
    cache_krope_t = jnp.swapaxes(cache_krope, 1, 2)
    for j in range(n_b):
        l = n_a + j
        q_lat, q_rope, z = _mla_query(xs, row(pre_norm_g[l]), w_in_mla_b[j], row(q_norm_g[j]), w_uq_pad[j],
                                      tabs_qs, w_ukt_pad, nseq=1, tile=rs)
        o_lat = _sample_attn(page_table, q_lat, q_rope, c_new, kr_new, cache_ckv, cache_krope_t)
        xs = _mla_out(o_lat, z, xs, w_out_mla_b[j], row(post_norm_g[l]), w_uv_pad, tile=rs)
    y_sample = xs.reshape(bs, ls, d)
    ckv_sample = c_s.reshape(bs, ls, kv_lora)
    krope_sample = kr_s.reshape(bs, ls, QK_ROPE)

    return (y_prompt, y_sample, ckv_prompt, krope_prompt, conv_prompt, ckv_sample, krope_sample, conv_sample)
```

```python
import functools

import jax
import jax.numpy as jnp
from jax import lax
from jax.experimental import pallas as pl
from jax.experimental.pallas import tpu as pltpu

N_META = 16
N_HEADS = 16
QK_NOPE = 64
QK_ROPE = 32
V_HEAD = 64
ROPE_THETA = 10000.0
RMS_EPS = 1e-6
CONV_WIDTH = 3
SOFTMAX_SCALE = (QK_NOPE + QK_ROPE) ** -0.5
LOG2_E = 1.4426950408889634

LANES = 128
SUBLANES = 8
HALF_ROPE = QK_ROPE // 2
ROPE_LANE0 = QK_ROPE
VMEM_LIMIT = 56 * 1024 * 1024

ROW_TILE = 512
ATTN_TILE = 1024
ATTN_HEADS_PER_STEP = 4
ATTN_DIAG_SPLITS = 2
SAMPLE_ATTN_CHUNKS = 4

bf16 = jnp.bfloat16
f32 = jnp.float32


def _rms(x, g):
    return x * lax.rsqrt(jnp.mean(x * x, axis=-1, keepdims=True) + RMS_EPS) * g


def _dot(a, b):
    return jnp.dot(a, b, preferred_element_type=f32)


def _dot_nt(a, b):
    return lax.dot_general(a, b, (((1,), (1,)), ((), ())), preferred_element_type=f32)


def _rope_block(blk, cos, sin):
    return blk * cos + pltpu.roll(blk, HALF_ROPE, 1) * sin


def _params(*sem):
    return pltpu.CompilerParams(dimension_semantics=sem, vmem_limit_bytes=VMEM_LIMIT)


def _const_spec(shape):
    nd = len(shape)
    return pl.BlockSpec(shape, lambda *_: (0,) * nd)


def _conv_layer_kernel(x_ref, init_ref, pre_g_ref, post_g_ref, w_in_ref, cw_ref, w_out_ref,
                       xo_ref, st_ref, vbuf, *, tile, off, shift, c_dim):
    i = pl.program_id(1)

    @pl.when(i == 0)
    def _():
        vbuf[0:off, :] = init_ref[0]

    x = x_ref[...]
    h = _rms(x, pre_g_ref[...]).astype(bf16)

    def proj(k):
        return _dot(h, w_in_ref[:, k * c_dim:(k + 1) * c_dim])

    vbuf[off:off + tile, :] = proj(1) * proj(2)
    cw = cw_ref[...]
    y = cw[0:1] * vbuf[off - 2 * shift:off - 2 * shift + tile, :]
    y = y + cw[1:2] * vbuf[off - shift:off - shift + tile, :]
    y = y + cw[2:3] * vbuf[off:off + tile, :]
    z = proj(3)
    g = (proj(0) * y * jax.nn.silu(z)).astype(bf16)
    m = _dot(g, w_out_ref[...])
    xo_ref[...] = x + _rms(m, post_g_ref[...])

    @pl.when(i == pl.num_programs(1) - 1)
    def _():
        st_ref[0] = vbuf[off + tile - 2 * shift:off + tile, :]

    vbuf[0:off, :] = vbuf[tile:tile + off, :]


def _conv_layer(x, init, pre_g, post_g, w_in, cw, w_out, *, nseq, tile, shift):
    rows, d = x.shape
    c_dim = cw.shape[1]
    t = rows // nseq
    nt = t // tile
    off = init.shape[1]
    ninit = init.shape[0]
    kern = functools.partial(_conv_layer_kernel, tile=tile, off=off, shift=shift, c_dim=c_dim)
    return pl.pallas_call(
        kern,
        grid=(nseq, nt),
        in_specs=[
            pl.BlockSpec((tile, d), lambda b, i: (b * nt + i, 0)),
            pl.BlockSpec((1, off, c_dim), (lambda b, i: (b, 0, 0)) if ninit > 1 else (lambda b, i: (0, 0, 0))),
            _const_spec((1, d)), _const_spec((1, d)),
            _const_spec(w_in.shape), _const_spec(cw.shape), _const_spec(w_out.shape),
        ],
        out_specs=[
            pl.BlockSpec((tile, d), lambda b, i: (b * nt + i, 0)),
            pl.BlockSpec((1, 2 * shift, c_dim), lambda b, i: (b, 0, 0)),
        ],
        out_shape=[jax.ShapeDtypeStruct((rows, d), f32),
                   jax.ShapeDtypeStruct((nseq, 2 * shift, c_dim), f32)],
        scratch_shapes=[pltpu.VMEM((off + tile, c_dim), f32)],
        compiler_params=_params("arbitrary", "arbitrary"),
        name="conv_layer",
    )(x, init, pre_g, post_g, w_in, cw, w_out)


def _latent_kernel(*refs, kv_lora, with_kv):
    if with_kv:
        (x_ref, g_ref, w_dkv_ref, lat_g_ref, cos_ref, sin_ref, w_uk_ref, w_uv_ref,
         c_ref, kr_ref, k_ref, v_ref) = refs
    else:
        x_ref, g_ref, w_dkv_ref, lat_g_ref, cos_ref, sin_ref, c_ref, kr_ref = refs
    xn = _rms(x_ref[...], g_ref[...]).astype(bf16)
    ckr = _dot(xn, w_dkv_ref[...])
    c = _rms(ckr[:, :kv_lora], lat_g_ref[...])
    krb = _rope_block(ckr[:, kv_lora:kv_lora + LANES], cos_ref[...], sin_ref[...])
    c_ref[...] = c
    kr_ref[...] = krb[:, ROPE_LANE0:ROPE_LANE0 + QK_ROPE]
    if with_kv:
        cb = c.astype(bf16)
        kn = _dot(cb, w_uk_ref[...])
        for h in range(N_HEADS):
            k_ref[:, h * LANES:(h + 1) * LANES] = (kn[:, h * LANES:(h + 1) * LANES] + krb).astype(bf16)
        pos = lax.broadcasted_iota(jnp.int32, (1, N_HEADS * LANES), 1) % (2 * LANES)
        ones_lanes = jnp.where(pos < V_HEAD, 0.0, jnp.where(pos < V_HEAD + LANES, 1.0, 0.0)).astype(f32)
        v_ref[...] = (_dot(cb, w_uv_ref[...]) + ones_lanes).astype(bf16)


def _latent(x, g, w_dkv_pad, lat_g, tabs, w_uk_pad=None, w_uv=None, *, t_out, tile):
    nseq, t_in, d = x.shape
    kv_lora = lat_g.shape[1]
    nt = t_in // tile
    with_kv = w_uk_pad is not None
    row_spec = lambda w: pl.BlockSpec((None, tile, w), lambda b, i: (b, i, 0))
    tab_spec = pl.BlockSpec((tile, LANES), lambda b, i: (i, 0))
    in_specs = [row_spec(d), _const_spec((1, d)), _const_spec(w_dkv_pad.shape), _const_spec((1, kv_lora)),
                tab_spec, tab_spec]
    args = [x, g, w_dkv_pad, lat_g, *tabs]
    out_specs = [row_spec(kv_lora), row_spec(QK_ROPE)]
    out_shape = [jax.ShapeDtypeStruct((nseq, t_out, kv_lora), f32), jax.ShapeDtypeStruct((nseq, t_out, QK_ROPE), f32)]
    if with_kv:
        in_specs += [_const_spec(w_uk_pad.shape), _const_spec(w_uv.shape)]
        args += [w_uk_pad, w_uv]
        out_specs += [row_spec(N_HEADS * LANES), row_spec(N_HEADS * LANES)]
        out_shape += [jax.ShapeDtypeStruct((nseq, t_out, N_HEADS * LANES), bf16),
                      jax.ShapeDtypeStruct((nseq, t_out, N_HEADS * LANES), bf16)]
    return pl.pallas_call(
        functools.partial(_latent_kernel, kv_lora=kv_lora, with_kv=with_kv),
        grid=(nseq, nt), in_specs=in_specs, out_specs=out_specs, out_shape=out_shape,
        compiler_params=_params("arbitrary", "arbitrary"),
        name="latent_kv" if with_kv else "latent",
    )(*args)


def _mla_query_kernel(*refs, q_lora, absorbed, rows):
    if absorbed:
        (x_ref, pre_g_ref, w_in_ref, qg_ref, w_uq_ref, cos_ref, sin_ref, w_ukt_ref,
         ql_ref, qr_ref, z_ref) = refs
    else:
        x_ref, pre_g_ref, w_in_ref, qg_ref, w_uq_ref, cos_ref, sin_ref, q_ref, z_ref = refs
    h = _rms(x_ref[...], pre_g_ref[...]).astype(bf16)
    q_lat = _dot(h, w_in_ref[:, :q_lora])
    z_ref[...] = _dot(h, w_in_ref[:, q_lora:]).astype(z_ref.dtype)
    qn = _rms(q_lat, qg_ref[...]).astype(bf16)
    q = _dot(qn, w_uq_ref[...])
    cos, sin = cos_ref[...], sin_ref[...]
    for hd in range(N_HEADS):
        blk = _rope_block(q[:, hd * LANES:(hd + 1) * LANES], cos, sin)
        if absorbed:
            ql = _dot(blk.astype(bf16), w_ukt_ref[hd])
            for c in range(ql.shape[1] // LANES):
                ql_ref[c, pl.ds(hd, rows, stride=N_HEADS), :] = ql[:, c * LANES:(c + 1) * LANES]
            qr_ref[pl.ds(hd, rows, stride=N_HEADS), :] = blk
        else:
            q_ref[:, hd * LANES:(hd + 1) * LANES] = blk.astype(bf16)


def _mla_query(x, pre_g, w_in, qg, w_uq_pad, tabs, w_ukt_pad=None, *, nseq, tile):
    rows, d = x.shape
    q_lora = qg.shape[1]
    z_dim = w_in.shape[1] - q_lora
    nt = rows // nseq // tile
    absorbed = w_ukt_pad is not None
    row_spec = lambda w: pl.BlockSpec((tile, w), lambda b, i: (b * nt + i, 0))
    tab_spec = pl.BlockSpec((tile, LANES), lambda b, i: (i, 0))
    in_specs = [row_spec(d), _const_spec((1, d)), _const_spec(w_in.shape), _const_spec((1, q_lora)),
                _const_spec(w_uq_pad.shape), tab_spec, tab_spec]
    args = [x, pre_g, w_in, qg, w_uq_pad, *tabs]
    if absorbed:
        assert nseq == 1 and nt == 1
        kv_lora = w_ukt_pad.shape[2]
        in_specs.append(_const_spec(w_ukt_pad.shape))
        args.append(w_ukt_pad)
        ql_shape = (kv_lora // LANES, rows * N_HEADS, LANES)
        out_specs = [_const_spec(ql_shape), _const_spec((rows * N_HEADS, LANES)), row_spec(z_dim)]
        out_shape = [jax.ShapeDtypeStruct(ql_shape, f32),
                     jax.ShapeDtypeStruct((rows * N_HEADS, LANES), f32),
                     jax.ShapeDtypeStruct((rows, z_dim), bf16)]
    else:
        out_specs = [row_spec(N_HEADS * LANES), row_spec(z_dim)]
        out_shape = [jax.ShapeDtypeStruct((rows, N_HEADS * LANES), bf16),
                     jax.ShapeDtypeStruct((rows, z_dim), bf16)]
    return pl.pallas_call(
        functools.partial(_mla_query_kernel, q_lora=q_lora, absorbed=absorbed, rows=rows),
        grid=(nseq, nt), in_specs=in_specs, out_specs=out_specs, out_shape=out_shape,
        compiler_params=_params("arbitrary", "arbitrary"),
        name="mla_query_absorbed" if absorbed else "mla_query",
    )(*args)


def _prompt_attn_kernel(q_ref, k_ref, v_ref, o_ref, acc_ref, *, tile):
    i = pl.program_id(2)
    neg = jnp.finfo(f32).min
    heads = range(q_ref.shape[2] // LANES)
    n_sub = ATTN_DIAG_SPLITS
    sub = tile // n_sub

    def keys(hh, start, size):
        return k_ref[0, pl.ds(start, size), hh * LANES:(hh + 1) * LANES]

    def values(hh, start, size):
        return v_ref[0, pl.ds(start, size), hh * LANES:(hh + 1) * LANES]

    d0 = pl.multiple_of(i * tile, tile)
    head_rows = sub - N_META

    def init_rows(hh, r):
        q = q_ref[0, r * sub:(r + 1) * sub, hh * LANES:(hh + 1) * LANES]
        nk = (r + 1) * sub
        row = lax.broadcasted_iota(jnp.int32, (sub, nk), 0) + (r * sub + N_META)
        col = lax.broadcasted_iota(jnp.int32, (sub, nk), 1)
        s = jnp.where(col <= row, _dot_nt(q, keys(hh, d0, nk)), neg)
        m = jnp.max(s, axis=-1, keepdims=True)
        acc = _dot(jnp.exp2(s - m).astype(bf16), values(hh, d0, nk))
        sc = _dot_nt(q[head_rows:], keys(hh, d0 + nk, N_META))
        rowc = lax.broadcasted_iota(jnp.int32, (N_META, N_META), 0)
        colc = lax.broadcasted_iota(jnp.int32, (N_META, N_META), 1)
        sc = jnp.where(colc <= rowc, sc, neg)
        m_c = jnp.max(sc, axis=-1, keepdims=True)
        acc_c = _dot(jnp.exp2(sc - m_c).astype(bf16), values(hh, d0 + nk, N_META))
        m_tail = jnp.maximum(m[head_rows:], m_c)
        acc_tail = jnp.exp2(m[head_rows:] - m_tail) * acc[head_rows:] + jnp.exp2(m_c - m_tail) * acc_c
        acc_ref[hh, r * sub:r * sub + head_rows, :] = acc[:head_rows]
        acc_ref[hh, r * sub + head_rows:(r + 1) * sub, :] = acc_tail
        return jnp.concatenate([jnp.broadcast_to(m[:head_rows], (head_rows, LANES)),
                                jnp.broadcast_to(m_tail, (N_META, LANES))], axis=0)

    ms = []
    for hh in heads:
        m_parts = [init_rows(hh, r) for r in range(n_sub)]
        ms.append(jnp.max(jnp.concatenate(m_parts, axis=0), axis=-1, keepdims=True))

    qs = [q_ref[0, :, hh * LANES:(hh + 1) * LANES] for hh in heads]

    def step(j, ms):
        start = pl.multiple_of(j * tile, tile)
        out = []
        for hh in heads:
            s = _dot_nt(qs[hh], keys(hh, start, tile))
            m_new = jnp.maximum(ms[hh], jnp.max(s, axis=-1, keepdims=True))
            alpha = jnp.exp2(ms[hh] - m_new)
            p = jnp.exp2(s - m_new)
            acc_ref[hh] = alpha * acc_ref[hh] + _dot(p.astype(bf16), values(hh, start, tile))
            out.append(m_new)
        return tuple(out)

    lax.fori_loop(0, i, step, tuple(ms))
    lane = lax.broadcasted_iota(jnp.int32, (tile, LANES), 1)
    for g in range(len(heads) // 2):
        a_even, a_odd = acc_ref[2 * g], acc_ref[2 * g + 1]
        num = jnp.where(lane < V_HEAD, a_even, a_odd)
        den = pltpu.roll(jnp.where(lane < V_HEAD, a_odd, a_even), V_HEAD, 1)
        o_ref[0, :, g * LANES:(g + 1) * LANES] = (num / den).astype(o_ref.dtype)


def _prompt_attn(q, k, v, *, tile):
    b, t, _ = q.shape
    tk = k.shape[1]
    nq = t // tile
    hps = ATTN_HEADS_PER_STEP
    return pl.pallas_call(
        functools.partial(_prompt_attn_kernel, tile=tile),
        grid=(b, N_HEADS // hps, nq),
        in_specs=[
            pl.BlockSpec((1, tile, hps * LANES), lambda b, g, i: (b, i, g)),
            pl.BlockSpec((1, tk, hps * LANES), lambda b, g, i: (b, 0, g)),
            pl.BlockSpec((1, tk, hps * LANES), lambda b, g, i: (b, 0, g)),
        ],
        out_specs=pl.BlockSpec((1, tile, hps * V_HEAD), lambda b, g, i: (b, i, g)),
        out_shape=jax.ShapeDtypeStruct((b, t, N_HEADS * V_HEAD), bf16),
        scratch_shapes=[pltpu.VMEM((hps, tile, LANES), f32)],
        compiler_params=_params("arbitrary", "arbitrary", "arbitrary"),
        name="prompt_attn",
    )(q, k, v)


def _sample_attn_kernel(pt_ref, ql_ref, qr_ref, cn_ref, krn_ref, ckv_hbm, krt_hbm, o_ref,
                        kv_land, kr_land, kb16, kr16, sem_kv, sem_kr, *, n_pages, page, n_chunks):
    b = pl.program_id(0)
    slot = lax.rem(b, 2)

    def page_copies(req, slot_, p):
        pg = pt_ref[req * n_pages + p]
        return (pltpu.make_async_copy(ckv_hbm.at[pg], kv_land.at[slot_, p], sem_kv.at[slot_]),
                pltpu.make_async_copy(krt_hbm.at[pg], kr_land.at[slot_, p], sem_kr.at[slot_]))

    def wait_slot(slot_):
        pltpu.make_async_copy(ckv_hbm.at[pl.ds(0, n_pages)], kv_land.at[slot_], sem_kv.at[slot_]).wait()
        pltpu.make_async_copy(krt_hbm.at[pl.ds(0, n_pages)], kr_land.at[slot_], sem_kr.at[slot_]).wait()

    @pl.when(b == 0)
    def _():
        def body(p, c):
            for cp in page_copies(0, 0, p):
                cp.start()
            return c
        lax.fori_loop(0, n_pages, body, 0)

    wait_slot(slot)
    nxt = jnp.minimum(b + 1, pl.num_programs(0) - 1)
    for p in range(n_pages):
        for cp in page_copies(nxt, 1 - slot, p):
            cp.start(priority=p % 2)

    n_half = ql_ref.shape[0]
    ql = jnp.concatenate([ql_ref[c] for c in range(n_half)], axis=-1).astype(bf16)
    qr = qr_ref[:, ROPE_LANE0:ROPE_LANE0 + QK_ROPE].astype(bf16)
    nq = ql.shape[0]

    ppc = n_pages // n_chunks
    parts = []
    for c in range(n_chunks):
        for p in range(c * ppc, (c + 1) * ppc):
            kb16[p * page:(p + 1) * page, :] = kv_land[slot, p].astype(bf16)
            kr16[:, p * page:(p + 1) * page] = kr_land[slot, p].astype(bf16)
        kb = kb16[c * ppc * page:(c + 1) * ppc * page, :]
        half = ppc * page // 2
        s = jnp.concatenate([_dot_nt(ql, kb[:half]), _dot_nt(ql, kb[half:])], axis=-1)
        s = s + _dot(qr, kr16[:, c * ppc * page:(c + 1) * ppc * page])
        m_c = jnp.max(s, axis=-1, keepdims=True)
        p_c = jnp.exp2(s - m_c)
        pb = p_c.astype(bf16)
        a_c = _dot(pb[:, :half], kb[:half]) + _dot(pb[:, half:], kb[half:])
        parts.append((m_c, jnp.sum(p_c, axis=-1, keepdims=True), a_c))
    cn = cn_ref[0].astype(bf16)
    krn = krn_ref[0].astype(bf16)
    npad = cn.shape[0]
    s_new = _dot_nt(ql, cn) + _dot_nt(qr, krn)
    row = lax.broadcasted_iota(jnp.int32, (nq, npad), 0)
    col = lax.broadcasted_iota(jnp.int32, (nq, npad), 1)
    s_new = jnp.where(col * N_HEADS <= row, s_new, jnp.finfo(f32).min)
    m_n = jnp.max(s_new, axis=-1, keepdims=True)
    p_n = jnp.exp2(s_new - m_n)
    parts.append((m_n, jnp.sum(p_n, axis=-1, keepdims=True), _dot(p_n.astype(bf16), cn)))

    m = functools.reduce(jnp.maximum, [pt_[0] for pt_ in parts])
    l = sum(jnp.exp2(m_c - m) * l_c for m_c, l_c, _ in parts)
    acc = sum(jnp.exp2(m_c - m) * a_c for m_c, _, a_c in parts)
    o = acc / l
    for c in range(n_half):
        o_ref[c] = o[:, c * LANES:(c + 1) * LANES]

    @pl.when(b == pl.num_programs(0) - 1)
    def _():
        wait_slot(1 - slot)


def _sample_attn(page_table, q_lat, q_rope, c_new, kr_new, cache_ckv, cache_krope_t):
    nreq, n_pages = page_table.shape
    _, page, kv_lora = cache_ckv.shape
    n_half = q_lat.shape[0]
    nq = q_lat.shape[1] // nreq
    lpad = c_new.shape[1]
    grid_spec = pltpu.PrefetchScalarGridSpec(
        num_scalar_prefetch=1, grid=(nreq,),
        in_specs=[
            pl.BlockSpec((n_half, nq, LANES), lambda b, pt: (0, b, 0)),
            pl.BlockSpec((nq, LANES), lambda b, pt: (b, 0)),
            pl.BlockSpec((1, lpad, kv_lora), lambda b, pt: (b, 0, 0)),
            pl.BlockSpec((1, lpad, QK_ROPE), lambda b, pt: (b, 0, 0)),
            pl.BlockSpec(memory_space=pl.ANY),
            pl.BlockSpec(memory_space=pl.ANY),
        ],
        out_specs=pl.BlockSpec((n_half, nq, LANES), lambda b, pt: (0, b, 0)),
        scratch_shapes=[
            pltpu.VMEM((2, n_pages, page, kv_lora), f32),
            pltpu.VMEM((2, n_pages, QK_ROPE, page), f32),
            pltpu.VMEM((n_pages * page, kv_lora), bf16),
            pltpu.VMEM((QK_ROPE, n_pages * page), bf16),
            pltpu.SemaphoreType.DMA((2,)),
            pltpu.SemaphoreType.DMA((2,)),
        ],
    )
    return pl.pallas_call(
        functools.partial(_sample_attn_kernel, n_pages=n_pages, page=page,
                          n_chunks=min(SAMPLE_ATTN_CHUNKS, n_pages)),
        grid_spec=grid_spec,
        out_shape=jax.ShapeDtypeStruct((n_half, nreq * nq, LANES), f32),
        compiler_params=_params("arbitrary"),
        name="sample_attn",
    )(page_table.reshape(-1), q_lat, q_rope, c_new, kr_new, cache_ckv, cache_krope_t)


def _mla_out_kernel(*refs, absorbed, rows):
    if absorbed:
        ol_ref, w_uv_ref, z_ref, x_ref, w_out_ref, post_g_ref, xo_ref = refs
        def head_rows(hd):
            halves = [ol_ref[c, pl.ds(hd, rows, stride=N_HEADS), :] for c in range(ol_ref.shape[0])]
            return jnp.concatenate(halves, axis=-1).astype(bf16)

        parts = []
        for g in range(N_HEADS // 2):
            parts.append(_dot(head_rows(2 * g), w_uv_ref[2 * g]) + _dot(head_rows(2 * g + 1), w_uv_ref[2 * g + 1]))
        o = jnp.concatenate(parts, axis=-1)
    else:
        o_ref, z_ref, x_ref, w_out_ref, post_g_ref, xo_ref = refs
        o = o_ref[...]
    g = (o.astype(f32) * jax.nn.silu(z_ref[...].astype(f32))).astype(bf16)
    m = _dot(g, w_out_ref[...])
    xo_ref[...] = x_ref[...] + _rms(m, post_g_ref[...])


def _mla_out(o, z, x, w_out, post_g, w_uv_pad=None, *, tile):
    rows, d = x.shape
    z_dim = z.shape[1]
    nt = rows // tile
    absorbed = w_uv_pad is not None
    row_spec = lambda w: pl.BlockSpec((tile, w), lambda i: (i, 0))
    if absorbed:
        assert nt == 1
        in_specs = [_const_spec(o.shape), _const_spec(w_uv_pad.shape)]
        args = [o, w_uv_pad]
    else:
        in_specs = [row_spec(z_dim)]
        args = [o]
    in_specs += [row_spec(z_dim), row_spec(d), _const_spec(w_out.shape), _const_spec((1, d))]
    args += [z, x, w_out, post_g]
    return pl.pallas_call(
        functools.partial(_mla_out_kernel, absorbed=absorbed, rows=rows),
        grid=(nt,), in_specs=in_specs, out_specs=row_spec(d),
        out_shape=jax.ShapeDtypeStruct((rows, d), f32),
        compiler_params=_params("arbitrary"),
        name="mla_out_absorbed" if absorbed else "mla_out",
    )(*args)


def _rope_tables(pos, scale):
    inv = ROPE_THETA ** (-jnp.arange(HALF_ROPE, dtype=f32) / HALF_ROPE)
    ang = pos.astype(f32)[:, None] * inv[None, :]
    cos, sin = jnp.cos(ang), jnp.sin(ang)
    r = pos.shape[0]
    zeros = lambda w: jnp.zeros((r, w), f32)
    c = jnp.concatenate([zeros(ROPE_LANE0), cos, cos, jnp.ones((r, QK_NOPE), f32)], axis=-1)
    s = jnp.concatenate([zeros(ROPE_LANE0), -sin, sin, zeros(QK_NOPE)], axis=-1)
    return c * scale, s * scale


def _head_blocks(w_rope, w_nope):
    blk = jnp.concatenate([w_rope, w_rope, w_nope], axis=-1)
    return blk.reshape(blk.shape[:-2] + (N_HEADS * LANES,))


def kernel(x_prompt, x_sample, cache_ckv, cache_krope, state_conv, page_table, meta_tokens,
           pre_norm_g, post_norm_g, w_in_conv, conv_w, w_out_conv, kv_norm_g, w_dkv,
           kv_lat_norm_g, w_uk, w_uv, w_in_mla, q_norm_g, w_uq, w_out_mla):
    bp, seq, d = x_prompt.shape
    bs, ls, _ = x_sample.shape
    n_a = w_in_conv.shape[0]
    n_b = w_in_mla.shape[0]
    c_dim = conv_w.shape[2]
    kv_lora = kv_lat_norm_g.shape[0]
    q_lora = q_norm_g.shape[1]
    past_len = page_table.shape[1] * cache_ckv.shape[1]

    row = lambda v: v.reshape(1, -1).astype(f32)
    w_in_conv_b = w_in_conv.astype(bf16)
    w_out_conv_b = w_out_conv.astype(bf16)
    w_in_mla_b = w_in_mla.astype(bf16)
    w_out_mla_b = w_out_mla.astype(bf16)
    w_dkv_pad = jnp.concatenate([w_dkv, w_dkv[:, kv_lora:], jnp.zeros((d, QK_NOPE), w_dkv.dtype)],
                                axis=-1).astype(bf16)
    uq = w_uq.reshape(n_b, q_lora, N_HEADS, QK_NOPE + QK_ROPE)
    w_uq_pad = _head_blocks(uq[..., QK_NOPE:], uq[..., :QK_NOPE]).astype(bf16)
    w_uk_blk = _head_blocks(jnp.zeros((kv_lora, N_HEADS, QK_ROPE), w_uk.dtype), w_uk)
    w_uk_pad = w_uk_blk.astype(bf16)
    w_ukt_pad = jnp.transpose(w_uk_blk.reshape(kv_lora, N_HEADS, LANES), (1, 2, 0)).astype(bf16)
    uv = jnp.transpose(w_uv, (1, 0, 2))
    zv = jnp.zeros_like(uv)
    even = (jnp.arange(N_HEADS) % 2 == 0)[:, None, None]
    w_uv_pad = jnp.concatenate([jnp.where(even, uv, zv), jnp.where(even, zv, uv)], axis=-1).astype(bf16)
    w_uv_blk = jnp.transpose(w_uv_pad, (1, 0, 2)).reshape(kv_lora, N_HEADS * LANES)

    def trunk_a(x, inits, *, nseq, tile, shift):
        states = []
        for l in range(n_a):
            x, st = _conv_layer(x, inits[l], row(pre_norm_g[l]), row(post_norm_g[l]), w_in_conv_b[l],
                                conv_w[l], w_out_conv_b[l], nseq=nseq, tile=tile, shift=shift)
            states.append(st)
        return x, states

    zero_init = jnp.zeros((1, SUBLANES, c_dim), f32)
    xm, meta_states = trunk_a(meta_tokens.astype(f32), [zero_init] * n_a, nseq=1, tile=N_META, shift=1)

    tile_p = min(ROW_TILE, seq)
    inits_p = [jnp.concatenate([jnp.zeros((1, SUBLANES - 2, c_dim), f32), st], axis=1) for st in meta_states]
    xp, prompt_states = trunk_a(x_prompt.reshape(bp * seq, d), inits_p, nseq=bp, tile=tile_p, shift=1)
    t_pos = N_META + seq
    t_pad = -(-t_pos // tile_p) * tile_p
    x_pos = jnp.concatenate([jnp.broadcast_to(xm[None], (bp, N_META, d)), xp.reshape(bp, seq, d),
                             jnp.zeros((bp, t_pad - t_pos, d), f32)], axis=1)
    tabs_pos = _rope_tables(jnp.arange(t_pad, dtype=jnp.int32), 1.0)
    ckv_prompt, krope_prompt, k_p, v_p = _latent(x_pos, row(kv_norm_g), w_dkv_pad, row(kv_lat_norm_g), tabs_pos,
                                                 w_uk_pad, w_uv_blk, t_out=t_pos, tile=tile_p)
    tabs_q = _rope_tables(N_META + jnp.arange(seq, dtype=jnp.int32), SOFTMAX_SCALE * LOG2_E)
    attn_tile = min(ATTN_TILE, seq)
    for j in range(n_b):
        l = n_a + j
        q, z = _mla_query(xp, row(pre_norm_g[l]), w_in_mla_b[j], row(q_norm_g[j]), w_uq_pad[j], tabs_q,
                          nseq=bp, tile=tile_p)
        o = _prompt_attn(q.reshape(bp, seq, -1), k_p, v_p, tile=attn_tile)
        xp = _mla_out(o.reshape(bp * seq, -1), z, xp, w_out_mla_b[j], row(post_norm_g[l]), tile=tile_p)
    y_prompt = xp.reshape(bp, seq, d)
    conv_prompt = jnp.stack([st for st in prompt_states])

    rs = bs * ls
    xs = jnp.transpose(x_sample, (1, 0, 2)).reshape(rs, d)
    inits_s = [jnp.transpose(state_conv[l], (1, 0, 2)).reshape(1, (CONV_WIDTH - 1) * bs, c_dim) for l in range(n_a)]
    xs, sample_states = trunk_a(xs, inits_s, nseq=1, tile=rs, shift=bs)
    conv_sample = jnp.stack([jnp.transpose(st.reshape(CONV_WIDTH - 1, bs, c_dim), (1, 0, 2)) for st in sample_states])
    xs = jnp.transpose(xs.reshape(ls, bs, d), (1, 0, 2)).reshape(rs, d)
    pos_s = jnp.tile(past_len + jnp.arange(ls, dtype=jnp.int32), bs)
    c_s, kr_s = _latent(xs[None], row(kv_norm_g), w_dkv_pad, row(kv_lat_norm_g), _rope_tables(pos_s, 1.0),
                        t_out=rs, tile=rs)
    lpad = 16
    c_new = jnp.pad(c_s.reshape(bs, ls, kv_lora), ((0, 0), (0, lpad - ls), (0, 0)))
    kr_new = jnp.pad(kr_s.reshape(bs, ls, QK_ROPE), ((0, 0), (0, lpad - ls), (0, 0)))
    tabs_qs = _rope_tables(pos_s, SOFTMAX_SCALE * LOG2_E)
    cache_krope_t = jnp.swapaxes(cache_krope, 1, 2)
    for j in range(n_b):
        l = n_a + j
        q_lat, q_rope, z = _mla_query(xs, row(pre_norm_g[l]), w_in_mla_b[j], row(q_norm_g[j]), w_uq_pad[j],
                                      tabs_qs, w_ukt_pad, nseq=1, tile=rs)
        o_lat = _sample_attn(page_table, q_lat, q_rope, c_new, kr_new, cache_ckv, cache_krope_t)
        xs = _mla_out(o_lat, z, xs, w_out_mla_b[j], row(post_norm_g[l]), w_uv_pad, tile=rs)
    y_sample = xs.reshape(bs, ls, d)
    ckv_sample = c_s.reshape(bs, ls, kv_lora)
    krope_sample = kr_s.reshape(bs, ls, QK_ROPE)

    return (y_prompt, y_sample, ckv_prompt, krope_prompt, conv_prompt, ckv_sample, krope_sample, conv_sample)
```

```python
import functools

import jax
import jax.numpy as jnp
from jax import lax
from jax.experimental import pallas as pl
from jax.experimental.pallas import tpu as pltpu

N_META = 16
N_HEADS = 16
QK_NOPE = 64
QK_ROPE = 32
V_HEAD = 64
ROPE_THETA = 10000.0
RMS_EPS = 1e-6
CONV_WIDTH = 3
SOFTMAX_SCALE = (QK_NOPE + QK_ROPE) ** -0.5
LOG2_E = 1.4426950408889634

LANES = 128
SUBLANES = 8
HALF_ROPE = QK_ROPE // 2
ROPE_LANE0 = QK_ROPE
VMEM_LIMIT = 56 * 1024 * 1024

ROW_TILE = 512
ATTN_TILE = 1024
ATTN_HEADS_PER_STEP = 4
ATTN_DIAG_SPLITS = 2
SAMPLE_ATTN_CHUNKS = 4

bf16 = jnp.bfloat16
f32 = jnp.float32


def _rms(x, g):
    return x * lax.rsqrt(jnp.mean(x * x, axis=-1, keepdims=True) + RMS_EPS) * g


def _dot(a, b):
    return jnp.dot(a, b, preferred_element_type=f32)


def _dot_nt(a, b):
    return lax.dot_general(a, b, (((1,), (1,)), ((), ())), preferred_element_type=f32)


def _rope_block(blk, cos, sin):
    return blk * cos + pltpu.roll(blk, HALF_ROPE, 1) * sin


def _params(*sem):
    return pltpu.CompilerParams(dimension_semantics=sem, vmem_limit_bytes=VMEM_LIMIT)


def _const_spec(shape):
    nd = len(shape)
    return pl.BlockSpec(shape, lambda *_: (0,) * nd)


def _layer_spec(stack, layer):
    nd = stack.ndim - 1
    return pl.BlockSpec((None,) + stack.shape[1:], lambda *_: (layer,) + (0,) * nd)


def _conv_layer_kernel(x_ref, init_ref, pre_g_ref, post_g_ref, w_in_ref, cw_ref, w_out_ref,
                       xo_ref, st_ref, vbuf, *, tile, off, shift, c_dim):
    i = pl.program_id(1)

    @pl.when(i == 0)
    def _():
        vbuf[0:off, :] = init_ref[0]

    x = x_ref[...]
    h = _rms(x, pre_g_ref[...]).astype(bf16)

    def proj(k):
        return _dot(h, w_in_ref[:, k * c_dim:(k + 1) * c_dim])

    vbuf[off:off + tile, :] = proj(1) * proj(2)
    cw = cw_ref[...]
    y = cw[0:1] * vbuf[off - 2 * shift:off - 2 * shift + tile, :]
    y = y + cw[1:2] * vbuf[off - shift:off - shift + tile, :]
    y = y + cw[2:3] * vbuf[off:off + tile, :]
    z = proj(3)
    g = (proj(0) * y * jax.nn.silu(z)).astype(bf16)
    m = _dot(g, w_out_ref[...])
    xo_ref[...] = x + _rms(m, post_g_ref[...])

    @pl.when(i == pl.num_programs(1) - 1)
    def _():
        st_ref[0] = vbuf[off + tile - 2 * shift:off + tile, :]

    vbuf[0:off, :] = vbuf[tile:tile + off, :]


def _conv_layer(x, init, pre_g, post_g, w_in, cw, w_out, *, layer, nseq, tile, shift):
    rows, d = x.shape
    c_dim = cw.shape[1]
    t = rows // nseq
    nt = t // tile
    off = init.shape[1]
    ninit = init.shape[0]
    kern = functools.partial(_conv_layer_kernel, tile=tile, off=off, shift=shift, c_dim=c_dim)
    return pl.pallas_call(
        kern,
        grid=(nseq, nt),
        in_specs=[
            pl.BlockSpec((tile, d), lambda b, i: (b * nt + i, 0)),
            pl.BlockSpec((1, off, c_dim), (lambda b, i: (b, 0, 0)) if ninit > 1 else (lambda b, i: (0, 0, 0))),
            _const_spec((1, d)), _const_spec((1, d)),
            _layer_spec(w_in, layer), _const_spec(cw.shape), _layer_spec(w_out, layer),
        ],
        out_specs=[
            pl.BlockSpec((tile, d), lambda b, i: (b * nt + i, 0)),
            pl.BlockSpec((1, 2 * shift, c_dim), lambda b, i: (b, 0, 0)),
        ],
        out_shape=[jax.ShapeDtypeStruct((rows, d), f32),
                   jax.ShapeDtypeStruct((nseq, 2 * shift, c_dim), f32)],
        scratch_shapes=[pltpu.VMEM((off + tile, c_dim), f32)],
        compiler_params=_params("arbitrary", "arbitrary"),
        name="conv_layer",
    )(x, init, pre_g, post_g, w_in, cw, w_out)


def _latent_kernel(*refs, kv_lora, with_kv):
    if with_kv:
        (x_ref, g_ref, w_dkv_ref, lat_g_ref, cos_ref, sin_ref, w_uk_ref, w_uv_ref,
         c_ref, kr_ref, k_ref, v_ref) = refs
    else:
        x_ref, g_ref, w_dkv_ref, lat_g_ref, cos_ref, sin_ref, c_ref, kr_ref = refs
    xn = _rms(x_ref[...], g_ref[...]).astype(bf16)
    ckr = _dot(xn, w_dkv_ref[...])
    c = _rms(ckr[:, :kv_lora], lat_g_ref[...])
    krb = _rope_block(ckr[:, kv_lora:kv_lora + LANES], cos_ref[...], sin_ref[...])
    c_ref[...] = c
    kr_ref[...] = krb[:, ROPE_LANE0:ROPE_LANE0 + QK_ROPE]
    if with_kv:
        cb = c.astype(bf16)
        kn = _dot(cb, w_uk_ref[...])
        for h in range(N_HEADS):
            k_ref[:, h * LANES:(h + 1) * LANES] = (kn[:, h * LANES:(h + 1) * LANES] + krb).astype(bf16)
        pos = lax.broadcasted_iota(jnp.int32, (1, N_HEADS * LANES), 1) % (2 * LANES)
        ones_lanes = jnp.where(pos < V_HEAD, 0.0, jnp.where(pos < V_HEAD + LANES, 1.0, 0.0)).astype(f32)
        v_ref[...] = (_dot(cb, w_uv_ref[...]) + ones_lanes).astype(bf16)


def _latent(x, g, w_dkv_pad, lat_g, tabs, w_uk_pad=None, w_uv=None, *, t_out, tile):
    nseq, t_in, d = x.shape
    kv_lora = lat_g.shape[1]
    nt = t_in // tile
    with_kv = w_uk_pad is not None
    row_spec = lambda w: pl.BlockSpec((None, tile, w), lambda b, i: (b, i, 0))
    tab_spec = pl.BlockSpec((tile, LANES), lambda b, i: (i, 0))
    in_specs = [row_spec(d), _const_spec((1, d)), _const_spec(w_dkv_pad.shape), _const_spec((1, kv_lora)),
                tab_spec, tab_spec]
    args = [x, g, w_dkv_pad, lat_g, *tabs]
    out_specs = [row_spec(kv_lora), row_spec(QK_ROPE)]
    out_shape = [jax.ShapeDtypeStruct((nseq, t_out, kv_lora), f32), jax.ShapeDtypeStruct((nseq, t_out, QK_ROPE), f32)]
    if with_kv:
        in_specs += [_const_spec(w_uk_pad.shape), _const_spec(w_uv.shape)]
        args += [w_uk_pad, w_uv]
        out_specs += [row_spec(N_HEADS * LANES), row_spec(N_HEADS * LANES)]
        out_shape += [jax.ShapeDtypeStruct((nseq, t_out, N_HEADS * LANES), bf16),
                      jax.ShapeDtypeStruct((nseq, t_out, N_HEADS * LANES), bf16)]
    return pl.pallas_call(
        functools.partial(_latent_kernel, kv_lora=kv_lora, with_kv=with_kv),
        grid=(nseq, nt), in_specs=in_specs, out_specs=out_specs, out_shape=out_shape,
        compiler_params=_params("arbitrary", "arbitrary"),
        name="latent_kv" if with_kv else "latent",
    )(*args)


def _mla_query_kernel(*refs, q_lora, absorbed, rows):
    if absorbed:
        (x_ref, pre_g_ref, w_in_ref, qg_ref, w_uq_ref, cos_ref, sin_ref, w_ukt_ref,
         ql_ref, qr_ref, z_ref) = refs
    else:
        x_ref, pre_g_ref, w_in_ref, qg_ref, w_uq_ref, cos_ref, sin_ref, q_ref, z_ref = refs
    h = _rms(x_ref[...], pre_g_ref[...]).astype(bf16)
    q_lat = _dot(h, w_in_ref[:, :q_lora])
    z_ref[...] = _dot(h, w_in_ref[:, q_lora:]).astype(z_ref.dtype)
    qn = _rms(q_lat, qg_ref[...]).astype(bf16)
    q = _dot(qn, w_uq_ref[...])
    cos, sin = cos_ref[...], sin_ref[...]
    for hd in range(N_HEADS):
        blk = _rope_block(q[:, hd * LANES:(hd + 1) * LANES], cos, sin)
        if absorbed:
            ql = _dot(blk.astype(bf16), w_ukt_ref[hd])
            for c in range(ql.shape[1] // LANES):
                ql_ref[c, pl.ds(hd, rows, stride=N_HEADS), :] = ql[:, c * LANES:(c + 1) * LANES]
            qr_ref[pl.ds(hd, rows, stride=N_HEADS), :] = blk
        else:
            q_ref[:, hd * LANES:(hd + 1) * LANES] = blk.astype(bf16)


def _mla_query(x, pre_g, w_in, qg, w_uq_pad, tabs, w_ukt_pad=None, *, layer, nseq, tile):
    rows, d = x.shape
    q_lora = qg.shape[1]
    z_dim = w_in.shape[2] - q_lora
    nt = rows // nseq // tile
    absorbed = w_ukt_pad is not None
    row_spec = lambda w: pl.BlockSpec((tile, w), lambda b, i: (b * nt + i, 0))
    tab_spec = pl.BlockSpec((tile, LANES), lambda b, i: (i, 0))
    in_specs = [row_spec(d), _const_spec((1, d)), _layer_spec(w_in, layer), _const_spec((1, q_lora)),
                _layer_spec(w_uq_pad, layer), tab_spec, tab_spec]
    args = [x, pre_g, w_in, qg, w_uq_pad, *tabs]
    if absorbed:
        assert nseq == 1 and nt == 1
        kv_lora = w_ukt_pad.shape[2]
        in_specs.append(_const_spec(w_ukt_pad.shape))
        args.append(w_ukt_pad)
        ql_shape = (kv_lora // LANES, rows * N_HEADS, LANES)
        out_specs = [_const_spec(ql_shape), _const_spec((rows * N_HEADS, LANES)), row_spec(z_dim)]
        out_shape = [jax.ShapeDtypeStruct(ql_shape, f32),
                     jax.ShapeDtypeStruct((rows * N_HEADS, LANES), f32),
                     jax.ShapeDtypeStruct((rows, z_dim), bf16)]
    else:
        out_specs = [row_spec(N_HEADS * LANES), row_spec(z_dim)]
        out_shape = [jax.ShapeDtypeStruct((rows, N_HEADS * LANES), bf16),
                     jax.ShapeDtypeStruct((rows, z_dim), bf16)]
    return pl.pallas_call(
        functools.partial(_mla_query_kernel, q_lora=q_lora, absorbed=absorbed, rows=rows),
        grid=(nseq, nt), in_specs=in_specs, out_specs=out_specs, out_shape=out_shape,
        compiler_params=_params("arbitrary", "arbitrary"),
        name="mla_query_absorbed" if absorbed else "mla_query",
    )(*args)


def _prompt_attn_kernel(q_ref, k_ref, v_ref, o_ref, acc_ref, *, tile):
    i = pl.program_id(2)
    neg = jnp.finfo(f32).min
    heads = range(q_ref.shape[2] // LANES)
    n_sub = ATTN_DIAG_SPLITS
    sub = tile // n_sub

    def keys(hh, start, size):
        return k_ref[0, pl.ds(start, size), hh * LANES:(hh + 1) * LANES]

    def values(hh, start, size):
        return v_ref[0, pl.ds(start, size), hh * LANES:(hh + 1) * LANES]

    d0 = pl.multiple_of(i * tile, tile)
    head_rows = sub - N_META

    def init_rows(hh, r):
        q = q_ref[0, r * sub:(r + 1) * sub, hh * LANES:(hh + 1) * LANES]
        nk = (r + 1) * sub
        row = lax.broadcasted_iota(jnp.int32, (sub, nk), 0) + (r * sub + N_META)
        col = lax.broadcasted_iota(jnp.int32, (sub, nk), 1)
        s = jnp.where(col <= row, _dot_nt(q, keys(hh, d0, nk)), neg)
        m = jnp.max(s, axis=-1, keepdims=True)
        acc = _dot(jnp.exp2(s - m).astype(bf16), values(hh, d0, nk))
        sc = _dot_nt(q[head_rows:], keys(hh, d0 + nk, N_META))
        rowc = lax.broadcasted_iota(jnp.int32, (N_META, N_META), 0)
        colc = lax.broadcasted_iota(jnp.int32, (N_META, N_META), 1)
        sc = jnp.where(colc <= rowc, sc, neg)
        m_c = jnp.max(sc, axis=-1, keepdims=True)
        acc_c = _dot(jnp.exp2(sc - m_c).astype(bf16), values(hh, d0 + nk, N_META))
        m_tail = jnp.maximum(m[head_rows:], m_c)
        acc_tail = jnp.exp2(m[head_rows:] - m_tail) * acc[head_rows:] + jnp.exp2(m_c - m_tail) * acc_c
        acc_ref[hh, r * sub:r * sub + head_rows, :] = acc[:head_rows]
        acc_ref[hh, r * sub + head_rows:(r + 1) * sub, :] = acc_tail
        return jnp.concatenate([jnp.broadcast_to(m[:head_rows], (head_rows, LANES)),
                                jnp.broadcast_to(m_tail, (N_META, LANES))], axis=0)

    ms = []
    for hh in heads:
        m_parts = [init_rows(hh, r) for r in range(n_sub)]
        ms.append(jnp.max(jnp.concatenate(m_parts, axis=0), axis=-1, keepdims=True))

    qs = [q_ref[0, :, hh * LANES:(hh + 1) * LANES] for hh in heads]

    def step(j, ms):
        start = pl.multiple_of(j * tile, tile)
        out = []
        for hh in heads:
            s = _dot_nt(qs[hh], keys(hh, start, tile))
            m_new = jnp.maximum(ms[hh], jnp.max(s, axis=-1, keepdims=True))
            alpha = jnp.exp2(ms[hh] - m_new)
            p = jnp.exp2(s - m_new)
            acc_ref[hh] = alpha * acc_ref[hh] + _dot(p.astype(bf16), values(hh, start, tile))
            out.append(m_new)
        return tuple(out)

    lax.fori_loop(0, i, step, tuple(ms))
    lane = lax.broadcasted_iota(jnp.int32, (tile, LANES), 1)
    for g in range(len(heads) // 2):
        a_even, a_odd = acc_ref[2 * g], acc_ref[2 * g + 1]
        num = jnp.where(lane < V_HEAD, a_even, a_odd)
        den = pltpu.roll(jnp.where(lane < V_HEAD, a_odd, a_even), V_HEAD, 1)
        o_ref[0, :, g * LANES:(g + 1) * LANES] = (num / den).astype(o_ref.dtype)


def _prompt_attn(q, k, v, *, tile):
    b, t, _ = q.shape
    tk = k.shape[1]
    nq = t // tile
    hps = ATTN_HEADS_PER_STEP
    return pl.pallas_call(
        functools.partial(_prompt_attn_kernel, tile=tile),
        grid=(b, N_HEADS // hps, nq),
        in_specs=[
            pl.BlockSpec((1, tile, hps * LANES), lambda b, g, i: (b, i, g)),
            pl.BlockSpec((1, tk, hps * LANES), lambda b, g, i: (b, 0, g)),
            pl.BlockSpec((1, tk, hps * LANES), lambda b, g, i: (b, 0, g)),
        ],
        out_specs=pl.BlockSpec((1, tile, hps * V_HEAD), lambda b, g, i: (b, i, g)),
        out_shape=jax.ShapeDtypeStruct((b, t, N_HEADS * V_HEAD), bf16),
        scratch_shapes=[pltpu.VMEM((hps, tile, LANES), f32)],
        compiler_params=_params("arbitrary", "arbitrary", "arbitrary"),
        name="prompt_attn",
    )(q, k, v)


def _sample_attn_kernel(pt_ref, ql_ref, qr_ref, cn_ref, krn_ref, ckv_hbm, krt_hbm, o_ref,
                        kv_land, kr_land, kb16, kr16, sem_kv, sem_kr, *, n_pages, page, n_chunks):
    b = pl.program_id(0)
    slot = lax.rem(b, 2)

    def page_copies(req, slot_, p):
        pg = pt_ref[req * n_pages + p]
        return (pltpu.make_async_copy(ckv_hbm.at[pg], kv_land.at[slot_, p], sem_kv.at[slot_]),
                pltpu.make_async_copy(krt_hbm.at[pg], kr_land.at[slot_, p], sem_kr.at[slot_]))

    def wait_slot(slot_):
        pltpu.make_async_copy(ckv_hbm.at[pl.ds(0, n_pages)], kv_land.at[slot_], sem_kv.at[slot_]).wait()
        pltpu.make_async_copy(krt_hbm.at[pl.ds(0, n_pages)], kr_land.at[slot_], sem_kr.at[slot_]).wait()

    @pl.when(b == 0)
    def _():
        def body(p, c):
            for cp in page_copies(0, 0, p):
                cp.start()
            return c
        lax.fori_loop(0, n_pages, body, 0)

    wait_slot(slot)
    nxt = jnp.minimum(b + 1, pl.num_programs(0) - 1)
    for p in range(n_pages):
        for cp in page_copies(nxt, 1 - slot, p):
            cp.start(priority=p % 2)

    n_half = ql_ref.shape[0]
    ql = jnp.concatenate([ql_ref[c] for c in range(n_half)], axis=-1).astype(bf16)
    qr = qr_ref[:, ROPE_LANE0:ROPE_LANE0 + QK_ROPE].astype(bf16)
    nq = ql.shape[0]

    ppc = n_pages // n_chunks
    parts = []
    for c in range(n_chunks):
        for p in range(c * ppc, (c + 1) * ppc):
            kb16[p * page:(p + 1) * page, :] = kv_land[slot, p].astype(bf16)
            kr16[:, p * page:(p + 1) * page] = kr_land[slot, p].astype(bf16)
        kb = kb16[c * ppc * page:(c + 1) * ppc * page, :]
        half = ppc * page // 2
        s = jnp.concatenate([_dot_nt(ql, kb[:half]), _dot_nt(ql, kb[half:])], axis=-1)
        s = s + _dot(qr, kr16[:, c * ppc * page:(c + 1) * ppc * page])
        m_c = jnp.max(s, axis=-1, keepdims=True)
        p_c = jnp.exp2(s - m_c)
        pb = p_c.astype(bf16)
        a_c = _dot(pb[:, :half], kb[:half]) + _dot(pb[:, half:], kb[half:])
        parts.append((m_c, jnp.sum(p_c, axis=-1, keepdims=True), a_c))
    cn = cn_ref[0].astype(bf16)
    krn = krn_ref[0].astype(bf16)
    npad = cn.shape[0]
    s_new = _dot_nt(ql, cn) + _dot_nt(qr, krn)
    row = lax.broadcasted_iota(jnp.int32, (nq, npad), 0)
    col = lax.broadcasted_iota(jnp.int32, (nq, npad), 1)
    s_new = jnp.where(col * N_HEADS <= row, s_new, jnp.finfo(f32).min)
    m_n = jnp.max(s_new, axis=-1, keepdims=True)
    p_n = jnp.exp2(s_new - m_n)
    parts.append((m_n, jnp.sum(p_n, axis=-1, keepdims=True), _dot(p_n.astype(bf16), cn)))

    m = functools.reduce(jnp.maximum, [pt_[0] for pt_ in parts])
    l = sum(jnp.exp2(m_c - m) * l_c for m_c, l_c, _ in parts)
    acc = sum(jnp.exp2(m_c - m) * a_c for m_c, _, a_c in parts)
    o = acc / l
    for c in range(n_half):
        o_ref[c] = o[:, c * LANES:(c + 1) * LANES]

    @pl.when(b == pl.num_programs(0) - 1)
    def _():
        wait_slot(1 - slot)


def _sample_attn(page_table, q_lat, q_rope, c_new, kr_new, cache_ckv, cache_krope_t):
    nreq, n_pages = page_table.shape
    _, page, kv_lora = cache_ckv.shape
    n_half = q_lat.shape[0]
    nq = q_lat.shape[1] // nreq
    lpad = c_new.shape[1]
    grid_spec = pltpu.PrefetchScalarGridSpec(
        num_scalar_prefetch=1, grid=(nreq,),
        in_specs=[
            pl.BlockSpec((n_half, nq, LANES), lambda b, pt: (0, b, 0)),
            pl.BlockSpec((nq, LANES), lambda b, pt: (b, 0)),
            pl.BlockSpec((1, lpad, kv_lora), lambda b, pt: (b, 0, 0)),
            pl.BlockSpec((1, lpad, QK_ROPE), lambda b, pt: (b, 0, 0)),
            pl.BlockSpec(memory_space=pl.ANY),
            pl.BlockSpec(memory_space=pl.ANY),
        ],
        out_specs=pl.BlockSpec((n_half, nq, LANES), lambda b, pt: (0, b, 0)),
        scratch_shapes=[
            pltpu.VMEM((2, n_pages, page, kv_lora), f32),
            pltpu.VMEM((2, n_pages, QK_ROPE, page), f32),
            pltpu.VMEM((n_pages * page, kv_lora), bf16),
            pltpu.VMEM((QK_ROPE, n_pages * page), bf16),
            pltpu.SemaphoreType.DMA((2,)),
            pltpu.SemaphoreType.DMA((2,)),
        ],
    )
    return pl.pallas_call(
        functools.partial(_sample_attn_kernel, n_pages=n_pages, page=page,
                          n_chunks=min(SAMPLE_ATTN_CHUNKS, n_pages)),
        grid_spec=grid_spec,
        out_shape=jax.ShapeDtypeStruct((n_half, nreq * nq, LANES), f32),
        compiler_params=_params("arbitrary"),
        name="sample_attn",
    )(page_table.reshape(-1), q_lat, q_rope, c_new, kr_new, cache_ckv, cache_krope_t)


def _mla_out_kernel(*refs, absorbed, rows):
    if absorbed:
        ol_ref, w_uv_ref, z_ref, x_ref, w_out_ref, post_g_ref, xo_ref = refs
        def head_rows(hd):
            halves = [ol_ref[c, pl.ds(hd, rows, stride=N_HEADS), :] for c in range(ol_ref.shape[0])]
            return jnp.concatenate(halves, axis=-1).astype(bf16)

        parts = []
        for g in range(N_HEADS // 2):
            parts.append(_dot(head_rows(2 * g), w_uv_ref[2 * g]) + _dot(head_rows(2 * g + 1), w_uv_ref[2 * g + 1]))
        o = jnp.concatenate(parts, axis=-1)
    else:
        o_ref, z_ref, x_ref, w_out_ref, post_g_ref, xo_ref = refs
        o = o_ref[...]
    g = (o.astype(f32) * jax.nn.silu(z_ref[...].astype(f32))).astype(bf16)
    m = _dot(g, w_out_ref[...])
    xo_ref[...] = x_ref[...] + _rms(m, post_g_ref[...])


def _mla_out(o, z, x, w_out, post_g, w_uv_pad=None, *, layer, tile):
    rows, d = x.shape
    z_dim = z.shape[1]
    nt = rows // tile
    absorbed = w_uv_pad is not None
    row_spec = lambda w: pl.BlockSpec((tile, w), lambda i: (i, 0))
    if absorbed:
        assert nt == 1
        in_specs = [_const_spec(o.shape), _const_spec(w_uv_pad.shape)]
        args = [o, w_uv_pad]
    else:
        in_specs = [row_spec(z_dim)]
        args = [o]
    in_specs += [row_spec(z_dim), row_spec(d), _layer_spec(w_out, layer), _const_spec((1, d))]
    args += [z, x, w_out, post_g]
    return pl.pallas_call(
        functools.partial(_mla_out_kernel, absorbed=absorbed, rows=rows),
        grid=(nt,), in_specs=in_specs, out_specs=row_spec(d),
        out_shape=jax.ShapeDtypeStruct((rows, d), f32),
        compiler_params=_params("arbitrary"),
        name="mla_out_absorbed" if absorbed else "mla_out",
    )(*args)


def _rope_tables(pos, scale):
    inv = ROPE_THETA ** (-jnp.arange(HALF_ROPE, dtype=f32) / HALF_ROPE)
    ang = pos.astype(f32)[:, None] * inv[None, :]
    cos, sin = jnp.cos(ang), jnp.sin(ang)
    r = pos.shape[0]
    zeros = lambda w: jnp.zeros((r, w), f32)
    c = jnp.concatenate([zeros(ROPE_LANE0), cos, cos, jnp.ones((r, QK_NOPE), f32)], axis=-1)
    s = jnp.concatenate([zeros(ROPE_LANE0), -sin, sin, zeros(QK_NOPE)], axis=-1)
    return c * scale, s * scale


def _head_blocks(w_rope, w_nope):
    blk = jnp.concatenate([w_rope, w_rope, w_nope], axis=-1)
    return blk.reshape(blk.shape[:-2] + (N_HEADS * LANES,))


def kernel(x_prompt, x_sample, cache_ckv, cache_krope, state_conv, page_table, meta_tokens,
           pre_norm_g, post_norm_g, w_in_conv, conv_w, w_out_conv, kv_norm_g, w_dkv,
           kv_lat_norm_g, w_uk, w_uv, w_in_mla, q_norm_g, w_uq, w_out_mla):
    bp, seq, d = x_prompt.shape
    bs, ls, _ = x_sample.shape
    n_a = w_in_conv.shape[0]
    n_b = w_in_mla.shape[0]
    c_dim = conv_w.shape[2]
    kv_lora = kv_lat_norm_g.shape[0]
    q_lora = q_norm_g.shape[1]
    past_len = page_table.shape[1] * cache_ckv.shape[1]

    row = lambda v: v.reshape(1, -1).astype(f32)
    w_in_conv_b = w_in_conv.astype(bf16)
    w_out_conv_b = w_out_conv.astype(bf16)
    w_in_mla_b = w_in_mla.astype(bf16)
    w_out_mla_b = w_out_mla.astype(bf16)
    w_dkv_pad = jnp.concatenate([w_dkv, w_dkv[:, kv_lora:], jnp.zeros((d, QK_NOPE), w_dkv.dtype)],
                                axis=-1).astype(bf16)
    uq = w_uq.reshape(n_b, q_lora, N_HEADS, QK_NOPE + QK_ROPE)
    w_uq_pad = _head_blocks(uq[..., QK_NOPE:], uq[..., :QK_NOPE]).astype(bf16)
    w_uk_blk = _head_blocks(jnp.zeros((kv_lora, N_HEADS, QK_ROPE), w_uk.dtype), w_uk)
    w_uk_pad = w_uk_blk.astype(bf16)
    w_ukt_pad = jnp.transpose(w_uk_blk.reshape(kv_lora, N_HEADS, LANES), (1, 2, 0)).astype(bf16)
    uv = jnp.transpose(w_uv, (1, 0, 2))
    zv = jnp.zeros_like(uv)
    even = (jnp.arange(N_HEADS) % 2 == 0)[:, None, None]
    w_uv_pad = jnp.concatenate([jnp.where(even, uv, zv), jnp.where(even, zv, uv)], axis=-1).astype(bf16)
    w_uv_blk = jnp.transpose(w_uv_pad, (1, 0, 2)).reshape(kv_lora, N_HEADS * LANES)

    def trunk_a(x, inits, *, nseq, tile, shift):
        states = []
        for l in range(n_a):
            x, st = _conv_layer(x, inits[l], row(pre_norm_g[l]), row(post_norm_g[l]), w_in_conv_b,
                                conv_w[l], w_out_conv_b, layer=l, nseq=nseq, tile=tile, shift=shift)
            states.append(st)
        return x, states

    zero_init = jnp.zeros((1, SUBLANES, c_dim), f32)
    xm, meta_states = trunk_a(meta_tokens.astype(f32), [zero_init] * n_a, nseq=1, tile=N_META, shift=1)

    tile_p = min(ROW_TILE, seq)
    inits_p = [jnp.concatenate([jnp.zeros((1, SUBLANES - 2, c_dim), f32), st], axis=1) for st in meta_states]
    xp, prompt_states = trunk_a(x_prompt.reshape(bp * seq, d), inits_p, nseq=bp, tile=tile_p, shift=1)
    t_pos = N_META + seq
    t_pad = -(-t_pos // tile_p) * tile_p
    x_pos = jnp.concatenate([jnp.broadcast_to(xm[None], (bp, N_META, d)), xp.reshape(bp, seq, d),
                             jnp.zeros((bp, t_pad - t_pos, d), f32)], axis=1)
    tabs_pos = _rope_tables(jnp.arange(t_pad, dtype=jnp.int32), 1.0)
    ckv_prompt, krope_prompt, k_p, v_p = _latent(x_pos, row(kv_norm_g), w_dkv_pad, row(kv_lat_norm_g), tabs_pos,
                                                 w_uk_pad, w_uv_blk, t_out=t_pos, tile=tile_p)
    tabs_q = tuple(t[N_META:t_pos] * (SOFTMAX_SCALE * LOG2_E) for t in tabs_pos)
    attn_tile = min(ATTN_TILE, seq)
    for j in range(n_b):
        l = n_a + j
        q, z = _mla_query(xp, row(pre_norm_g[l]), w_in_mla_b, row(q_norm_g[j]), w_uq_pad, tabs_q,
                          layer=j, nseq=bp, tile=tile_p)
        o = _prompt_attn(q.reshape(bp, seq, -1), k_p, v_p, tile=attn_tile)
        xp = _mla_out(o.reshape(bp * seq, -1), z, xp, w_out_mla_b, row(post_norm_g[l]), layer=j, tile=tile_p)
    y_prompt = xp.reshape(bp, seq, d)
    conv_prompt = jnp.stack([st for st in prompt_states])

    rs = bs * ls
    xs = jnp.transpose(x_sample, (1, 0, 2)).reshape(rs, d)
    inits_s = [jnp.transpose(state_conv[l], (1, 0, 2)).reshape(1, (CONV_WIDTH - 1) * bs, c_dim) for l in range(n_a)]
    xs, sample_states = trunk_a(xs, inits_s, nseq=1, tile=rs, shift=bs)
    conv_sample = jnp.stack([jnp.transpose(st.reshape(CONV_WIDTH - 1, bs, c_dim), (1, 0, 2)) for st in sample_states])
    xs = jnp.transpose(xs.reshape(ls, bs, d), (1, 0, 2)).reshape(rs, d)
    pos_s = jnp.tile(past_len + jnp.arange(ls, dtype=jnp.int32), bs)
    c_s, kr_s = _latent(xs[None], row(kv_norm_g), w_dkv_pad, row(kv_lat_norm_g), _rope_tables(pos_s, 1.0),
                        t_out=rs, tile=rs)
    lpad = 16
    c_new = jnp.pad(c_s.reshape(bs, ls, kv_lora), ((0, 0), (0, lpad - ls), (0, 0)))
    kr_new = jnp.pad(kr_s.reshape(bs, ls, QK_ROPE), ((0, 0), (0, lpad - ls), (0, 0)))
    tabs_qs = _rope_tables(pos_s, SOFTMAX_SCALE * LOG2_E)
    cache_krope_t = jnp.swapaxes(cache_krope, 1, 2)
    for j in range(n_b):
        l = n_a + j
        q_lat, q_rope, z = _mla_query(xs, row(pre_norm_g[l]), w_in_mla_b, row(q_norm_g[j]), w_uq_pad,
                                      tabs_qs, w_ukt_pad, layer=j, nseq=1, tile=rs)
        o_lat = _sample_attn(page_table, q_lat, q_rope, c_new, kr_new, cache_ckv, cache_krope_t)
        xs = _mla_out(o_lat, z, xs, w_out_mla_b, row(post_norm_g[l]), w_uv_pad, layer=j, tile=rs)
    y_sample = xs.reshape(bs, ls, d)
    ckv_sample = c_s.reshape(bs, ls, kv_lora)
    krope_sample = kr_s.reshape(bs, ls, QK_ROPE)

    return (y_prompt, y_sample, ckv_prompt, krope_prompt, conv_prompt, ckv_sample, krope_sample, conv_sample)
```

```python
import functools

import jax
import jax.numpy as jnp
from jax import lax
from jax.experimental import pallas as pl
from jax.experimental.pallas import tpu as pltpu

N_META = 16
N_HEADS = 16
QK_NOPE = 64
QK_ROPE = 32
V_HEAD = 64
ROPE_THETA = 10000.0
RMS_EPS = 1e-6
CONV_WIDTH = 3
SOFTMAX_SCALE = (QK_NOPE + QK_ROPE) ** -0.5
LOG2_E = 1.4426950408889634

LANES = 128
SUBLANES = 8
HALF_ROPE = QK_ROPE // 2
ROPE_LANE0 = QK_ROPE
VMEM_LIMIT = 56 * 1024 * 1024

ROW_TILE = 512
ATTN_TILE = 1024
ATTN_HEADS_PER_STEP = 4
ATTN_DIAG_SPLITS = 2
SAMPLE_ATTN_CHUNKS = 4

bf16 = jnp.bfloat16
f32 = jnp.float32


def _rms(x, g):
    return x * lax.rsqrt(jnp.mean(x * x, axis=-1, keepdims=True) + RMS_EPS) * g


def _dot(a, b):
    return jnp.dot(a, b, preferred_element_type=f32)


def _dot_nt(a, b):
    return lax.dot_general(a, b, (((1,), (1,)), ((), ())), preferred_element_type=f32)


def _rope_block(blk, cos, sin):
    return blk * cos + pltpu.roll(blk, HALF_ROPE, 1) * sin


def _params(*sem):
    return pltpu.CompilerParams(dimension_semantics=sem, vmem_limit_bytes=VMEM_LIMIT)


def _const_spec(shape):
    nd = len(shape)
    return pl.BlockSpec(shape, lambda *_: (0,) * nd)


def _layer_spec(stack, layer):
    nd = stack.ndim - 1
    return pl.BlockSpec((None,) + stack.shape[1:], lambda *_: (layer,) + (0,) * nd)


def _conv_layer_kernel(x_ref, init_ref, pre_g_ref, post_g_ref, w_in_ref, cw_ref, w_out_ref,
                       xo_ref, st_ref, vbuf, *, tile, off, shift, c_dim):
    i = pl.program_id(1)

    @pl.when(i == 0)
    def _():
        vbuf[0:off, :] = init_ref[0]

    x = x_ref[...]
    h = _rms(x, pre_g_ref[...]).astype(bf16)

    def proj(k):
        return _dot(h, w_in_ref[:, k * c_dim:(k + 1) * c_dim])

    vbuf[off:off + tile, :] = proj(1) * proj(2)
    cw = cw_ref[...]
    y = cw[0:1] * vbuf[off - 2 * shift:off - 2 * shift + tile, :]
    y = y + cw[1:2] * vbuf[off - shift:off - shift + tile, :]
    y = y + cw[2:3] * vbuf[off:off + tile, :]
    z = proj(3)
    g = (proj(0) * y * jax.nn.silu(z)).astype(bf16)
    m = _dot(g, w_out_ref[...])
    xo_ref[...] = x + _rms(m, post_g_ref[...])

    @pl.when(i == pl.num_programs(1) - 1)
    def _():
        st_ref[0] = vbuf[off + tile - 2 * shift:off + tile, :]

    vbuf[0:off, :] = vbuf[tile:tile + off, :]


def _conv_layer(x, init, pre_g, post_g, w_in, cw, w_out, *, layer, nseq, tile, shift):
    rows, d = x.shape
    c_dim = cw.shape[1]
    t = rows // nseq
    nt = t // tile
    off = init.shape[1]
    ninit = init.shape[0]
    kern = functools.partial(_conv_layer_kernel, tile=tile, off=off, shift=shift, c_dim=c_dim)
    return pl.pallas_call(
        kern,
        grid=(nseq, nt),
        in_specs=[
            pl.BlockSpec((tile, d), lambda b, i: (b * nt + i, 0)),
            pl.BlockSpec((1, off, c_dim), (lambda b, i: (b, 0, 0)) if ninit > 1 else (lambda b, i: (0, 0, 0))),
            _const_spec((1, d)), _const_spec((1, d)),
            _layer_spec(w_in, layer), _const_spec(cw.shape), _layer_spec(w_out, layer),
        ],
        out_specs=[
            pl.BlockSpec((tile, d), lambda b, i: (b * nt + i, 0)),
            pl.BlockSpec((1, 2 * shift, c_dim), lambda b, i: (b, 0, 0)),
        ],
        out_shape=[jax.ShapeDtypeStruct((rows, d), f32),
                   jax.ShapeDtypeStruct((nseq, 2 * shift, c_dim), f32)],
        scratch_shapes=[pltpu.VMEM((off + tile, c_dim), f32)],
        compiler_params=_params("arbitrary", "arbitrary"),
        name="conv_layer",
    )(x, init, pre_g, post_g, w_in, cw, w_out)


def _latent_kernel(*refs, kv_lora, with_kv):
    if with_kv:
        (x_ref, g_ref, w_dkv_ref, lat_g_ref, cos_ref, sin_ref, w_uk_ref, w_uv_ref,
         c_ref, kr_ref, k_ref, v_ref) = refs
    else:
        x_ref, g_ref, w_dkv_ref, lat_g_ref, cos_ref, sin_ref, c_ref, kr_ref = refs
    xn = _rms(x_ref[...], g_ref[...]).astype(bf16)
    ckr = _dot(xn, w_dkv_ref[...])
    c = _rms(ckr[:, :kv_lora], lat_g_ref[...])
    krb = _rope_block(ckr[:, kv_lora:kv_lora + LANES], cos_ref[...], sin_ref[...])
    c_ref[...] = c
    kr_ref[...] = krb[:, ROPE_LANE0:ROPE_LANE0 + QK_ROPE]
    if with_kv:
        cb = c.astype(bf16)
        kn = _dot(cb, w_uk_ref[...])
        for h in range(N_HEADS):
            k_ref[:, h * LANES:(h + 1) * LANES] = (kn[:, h * LANES:(h + 1) * LANES] + krb).astype(bf16)
        pos = lax.broadcasted_iota(jnp.int32, (1, N_HEADS * LANES), 1) % (2 * LANES)
        ones_lanes = jnp.where(pos < V_HEAD, 0.0, jnp.where(pos < V_HEAD + LANES, 1.0, 0.0)).astype(f32)
        v_ref[...] = (_dot(cb, w_uv_ref[...]) + ones_lanes).astype(bf16)


def _latent(x, g, w_dkv_pad, lat_g, tabs, w_uk_pad=None, w_uv=None, *, t_out, tile):
    nseq, t_in, d = x.shape
    kv_lora = lat_g.shape[1]
    nt = t_in // tile
    with_kv = w_uk_pad is not None
    row_spec = lambda w: pl.BlockSpec((None, tile, w), lambda b, i: (b, i, 0))
    tab_spec = pl.BlockSpec((tile, LANES), lambda b, i: (i, 0))
    in_specs = [row_spec(d), _const_spec((1, d)), _const_spec(w_dkv_pad.shape), _const_spec((1, kv_lora)),
                tab_spec, tab_spec]
    args = [x, g, w_dkv_pad, lat_g, *tabs]
    out_specs = [row_spec(kv_lora), row_spec(QK_ROPE)]
    out_shape = [jax.ShapeDtypeStruct((nseq, t_out, kv_lora), f32), jax.ShapeDtypeStruct((nseq, t_out, QK_ROPE), f32)]
    if with_kv:
        in_specs += [_const_spec(w_uk_pad.shape), _const_spec(w_uv.shape)]
        args += [w_uk_pad, w_uv]
        out_specs += [row_spec(N_HEADS * LANES), row_spec(N_HEADS * LANES)]
        out_shape += [jax.ShapeDtypeStruct((nseq, t_out, N_HEADS * LANES), bf16),
                      jax.ShapeDtypeStruct((nseq, t_out, N_HEADS * LANES), bf16)]
    return pl.pallas_call(
        functools.partial(_latent_kernel, kv_lora=kv_lora, with_kv=with_kv),
        grid=(nseq, nt), in_specs=in_specs, out_specs=out_specs, out_shape=out_shape,
        compiler_params=_params("arbitrary", "arbitrary"),
        name="latent_kv" if with_kv else "latent",
    )(*args)


def _mla_query_kernel(*refs, q_lora, absorbed, rows):
    if absorbed:
        (x_ref, pre_g_ref, w_in_ref, qg_ref, w_uq_ref, cos_ref, sin_ref, w_ukt_ref,
         ql_ref, qr_ref, z_ref) = refs
    else:
        x_ref, pre_g_ref, w_in_ref, qg_ref, w_uq_ref, cos_ref, sin_ref, q_ref, z_ref = refs
    h = _rms(x_ref[...], pre_g_ref[...]).astype(bf16)
    q_lat = _dot(h, w_in_ref[:, :q_lora])
    z_ref[...] = _dot(h, w_in_ref[:, q_lora:]).astype(z_ref.dtype)
    qn = _rms(q_lat, qg_ref[...]).astype(bf16)
    q = _dot(qn, w_uq_ref[...])
    cos, sin = cos_ref[...], sin_ref[...]
    for hd in range(N_HEADS):
        blk = _rope_block(q[:, hd * LANES:(hd + 1) * LANES], cos, sin)
        if absorbed:
            ql = _dot(blk.astype(bf16), w_ukt_ref[hd])
            for c in range(ql.shape[1] // LANES):
                ql_ref[c, pl.ds(hd, rows, stride=N_HEADS), :] = ql[:, c * LANES:(c + 1) * LANES]
            qr_ref[pl.ds(hd, rows, stride=N_HEADS), :] = blk
        else:
            q_ref[:, hd * LANES:(hd + 1) * LANES] = blk.astype(bf16)


def _mla_query(x, pre_g, w_in, qg, w_uq_pad, tabs, w_ukt_pad=None, *, layer, nseq, tile):
    rows, d = x.shape
    q_lora = qg.shape[1]
    z_dim = w_in.shape[2] - q_lora
    nt = rows // nseq // tile
    absorbed = w_ukt_pad is not None
    row_spec = lambda w: pl.BlockSpec((tile, w), lambda b, i: (b * nt + i, 0))
    tab_spec = pl.BlockSpec((tile, LANES), lambda b, i: (i, 0))
    in_specs = [row_spec(d), _const_spec((1, d)), _layer_spec(w_in, layer), _const_spec((1, q_lora)),
                _layer_spec(w_uq_pad, layer), tab_spec, tab_spec]
    args = [x, pre_g, w_in, qg, w_uq_pad, *tabs]
    if absorbed:
        assert nseq == 1 and nt == 1
        kv_lora = w_ukt_pad.shape[2]
        in_specs.append(_const_spec(w_ukt_pad.shape))
        args.append(w_ukt_pad)
        ql_shape = (kv_lora // LANES, rows * N_HEADS, LANES)
        out_specs = [_const_spec(ql_shape), _const_spec((rows * N_HEADS, LANES)), row_spec(z_dim)]
        out_shape = [jax.ShapeDtypeStruct(ql_shape, f32),
                     jax.ShapeDtypeStruct((rows * N_HEADS, LANES), f32),
                     jax.ShapeDtypeStruct((rows, z_dim), bf16)]
    else:
        out_specs = [row_spec(N_HEADS * LANES), row_spec(z_dim)]
        out_shape = [jax.ShapeDtypeStruct((rows, N_HEADS * LANES), bf16),
                     jax.ShapeDtypeStruct((rows, z_dim), bf16)]
    return pl.pallas_call(
        functools.partial(_mla_query_kernel, q_lora=q_lora, absorbed=absorbed, rows=rows),
        grid=(nseq, nt), in_specs=in_specs, out_specs=out_specs, out_shape=out_shape,
        compiler_params=_params("arbitrary", "arbitrary"),
        name="mla_query_absorbed" if absorbed else "mla_query",
    )(*args)


def _prompt_attn_kernel(q_ref, k_ref, v_ref, o_ref, acc_ref, *, tile):
    i = pl.program_id(2)
    neg = jnp.finfo(f32).min
    heads = range(q_ref.shape[2] // LANES)
    n_sub = ATTN_DIAG_SPLITS
    sub = tile // n_sub

    def keys(hh, start, size):
        return k_ref[0, pl.ds(start, size), hh * LANES:(hh + 1) * LANES]

    def values(hh, start, size):
        return v_ref[0, pl.ds(start, size), hh * LANES:(hh + 1) * LANES]

    d0 = pl.multiple_of(i * tile, tile)
    head_rows = sub - N_META

    def init_rows(hh, r):
        q = q_ref[0, r * sub:(r + 1) * sub, hh * LANES:(hh + 1) * LANES]
        nk = (r + 1) * sub
        row = lax.broadcasted_iota(jnp.int32, (sub, nk), 0) + (r * sub + N_META)
        col = lax.broadcasted_iota(jnp.int32, (sub, nk), 1)
        s = jnp.where(col <= row, _dot_nt(q, keys(hh, d0, nk)), neg)
        m = jnp.max(s, axis=-1, keepdims=True)
        acc = _dot(jnp.exp2(s - m).astype(bf16), values(hh, d0, nk))
        sc = _dot_nt(q[head_rows:], keys(hh, d0 + nk, N_META))
        rowc = lax.broadcasted_iota(jnp.int32, (N_META, N_META), 0)
        colc = lax.broadcasted_iota(jnp.int32, (N_META, N_META), 1)
        sc = jnp.where(colc <= rowc, sc, neg)
        m_c = jnp.max(sc, axis=-1, keepdims=True)
        acc_c = _dot(jnp.exp2(sc - m_c).astype(bf16), values(hh, d0 + nk, N_META))
        m_tail = jnp.maximum(m[head_rows:], m_c)
        acc_tail = jnp.exp2(m[head_rows:] - m_tail) * acc[head_rows:] + jnp.exp2(m_c - m_tail) * acc_c
        acc_ref[hh, r * sub:r * sub + head_rows, :] = acc[:head_rows]
        acc_ref[hh, r * sub + head_rows:(r + 1) * sub, :] = acc_tail
        return jnp.concatenate([jnp.broadcast_to(m[:head_rows], (head_rows, LANES)),
                                jnp.broadcast_to(m_tail, (N_META, LANES))], axis=0)

    ms = []
    for hh in heads:
        m_parts = [init_rows(hh, r) for r in reversed(range(n_sub))][::-1]
        ms.append(jnp.max(jnp.concatenate(m_parts, axis=0), axis=-1, keepdims=True))

    qs = [q_ref[0, :, hh * LANES:(hh + 1) * LANES] for hh in heads]

    def step(j, ms):
        start = pl.multiple_of(j * tile, tile)
        out = []
        for hh in heads:
            s = _dot_nt(qs[hh], keys(hh, start, tile))
            m_new = jnp.maximum(ms[hh], jnp.max(s, axis=-1, keepdims=True))
            alpha = jnp.exp2(ms[hh] - m_new)
            p = jnp.exp2(s - m_new)
            acc_ref[hh] = alpha * acc_ref[hh] + _dot(p.astype(bf16), values(hh, start, tile))
            out.append(m_new)
        return tuple(out)

    lax.fori_loop(0, i, step, tuple(ms))
    lane = lax.broadcasted_iota(jnp.int32, (tile, LANES), 1)
    for g in range(len(heads) // 2):
        a_even, a_odd = acc_ref[2 * g], acc_ref[2 * g + 1]
        num = jnp.where(lane < V_HEAD, a_even, a_odd)
        den = pltpu.roll(jnp.where(lane < V_HEAD, a_odd, a_even), V_HEAD, 1)
        o_ref[0, :, g * LANES:(g + 1) * LANES] = (num / den).astype(o_ref.dtype)


def _prompt_attn(q, k, v, *, tile):
    b, t, _ = q.shape
    tk = k.shape[1]
    nq = t // tile
    hps = ATTN_HEADS_PER_STEP
    return pl.pallas_call(
        functools.partial(_prompt_attn_kernel, tile=tile),
        grid=(b, N_HEADS // hps, nq),
        in_specs=[
            pl.BlockSpec((1, tile, hps * LANES), lambda b, g, i: (b, i, g)),
            pl.BlockSpec((1, tk, hps * LANES), lambda b, g, i: (b, 0, g)),
            pl.BlockSpec((1, tk, hps * LANES), lambda b, g, i: (b, 0, g)),
        ],
        out_specs=pl.BlockSpec((1, tile, hps * V_HEAD), lambda b, g, i: (b, i, g)),
        out_shape=jax.ShapeDtypeStruct((b, t, N_HEADS * V_HEAD), bf16),
        scratch_shapes=[pltpu.VMEM((hps, tile, LANES), f32)],
        compiler_params=_params("arbitrary", "arbitrary", "arbitrary"),
        name="prompt_attn",
    )(q, k, v)


def _sample_attn_kernel(pt_ref, ql_ref, qr_ref, cn_ref, krn_ref, ckv_hbm, krt_hbm, o_ref,
                        kv_land, kr_land, kb16, kr16, sem_kv, sem_kr, *, n_pages, page, n_chunks):
    b = pl.program_id(0)
    slot = lax.rem(b, 2)

    def page_copies(req, slot_, p):
        pg = pt_ref[req * n_pages + p]
        return (pltpu.make_async_copy(ckv_hbm.at[pg], kv_land.at[slot_, p], sem_kv.at[slot_]),
                pltpu.make_async_copy(krt_hbm.at[pg], kr_land.at[slot_, p], sem_kr.at[slot_]))

    def wait_slot(slot_):
        pltpu.make_async_copy(ckv_hbm.at[pl.ds(0, n_pages)], kv_land.at[slot_], sem_kv.at[slot_]).wait()
        pltpu.make_async_copy(krt_hbm.at[pl.ds(0, n_pages)], kr_land.at[slot_], sem_kr.at[slot_]).wait()

    @pl.when(b == 0)
    def _():
        def body(p, c):
            for cp in page_copies(0, 0, p):
                cp.start()
            return c
        lax.fori_loop(0, n_pages, body, 0)

    wait_slot(slot)
    nxt = jnp.minimum(b + 1, pl.num_programs(0) - 1)
    for p in range(n_pages):
        for cp in page_copies(nxt, 1 - slot, p):
            cp.start(priority=p % 2)

    n_half = ql_ref.shape[0]
    ql = jnp.concatenate([ql_ref[c] for c in range(n_half)], axis=-1).astype(bf16)
    qr = qr_ref[:, ROPE_LANE0:ROPE_LANE0 + QK_ROPE].astype(bf16)
    nq = ql.shape[0]

    ppc = n_pages // n_chunks
    parts = []
    for c in range(n_chunks):
        for p in range(c * ppc, (c + 1) * ppc):
            kb16[p * page:(p + 1) * page, :] = kv_land[slot, p].astype(bf16)
            kr16[:, p * page:(p + 1) * page] = kr_land[slot, p].astype(bf16)
        kb = kb16[c * ppc * page:(c + 1) * ppc * page, :]
        half = ppc * page // 2
        s = jnp.concatenate([_dot_nt(ql, kb[:half]), _dot_nt(ql, kb[half:])], axis=-1)
        s = s + _dot(qr, kr16[:, c * ppc * page:(c + 1) * ppc * page])
        m_c = jnp.max(s, axis=-1, keepdims=True)
        p_c = jnp.exp2(s - m_c)
        pb = p_c.astype(bf16)
        a_c = _dot(pb[:, :half], kb[:half]) + _dot(pb[:, half:], kb[half:])
        parts.append((m_c, jnp.sum(p_c, axis=-1, keepdims=True), a_c))
    cn = cn_ref[0].astype(bf16)
    krn = krn_ref[0].astype(bf16)
    npad = cn.shape[0]
    s_new = _dot_nt(ql, cn) + _dot_nt(qr, krn)
    row = lax.broadcasted_iota(jnp.int32, (nq, npad), 0)
    col = lax.broadcasted_iota(jnp.int32, (nq, npad), 1)
    s_new = jnp.where(col * N_HEADS <= row, s_new, jnp.finfo(f32).min)
    m_n = jnp.max(s_new, axis=-1, keepdims=True)
    p_n = jnp.exp2(s_new - m_n)
    parts.append((m_n, jnp.sum(p_n, axis=-1, keepdims=True), _dot(p_n.astype(bf16), cn)))

    m = functools.reduce(jnp.maximum, [pt_[0] for pt_ in parts])
    l = sum(jnp.exp2(m_c - m) * l_c for m_c, l_c, _ in parts)
    acc = sum(jnp.exp2(m_c - m) * a_c for m_c, _, a_c in parts)
    o = acc / l
    for c in range(n_half):
        o_ref[c] = o[:, c * LANES:(c + 1) * LANES]

    @pl.when(b == pl.num_programs(0) - 1)
    def _():
        wait_slot(1 - slot)


def _sample_attn(page_table, q_lat, q_rope, c_new, kr_new, cache_ckv, cache_krope_t):
    nreq, n_pages = page_table.shape
    _, page, kv_lora = cache_ckv.shape
    n_half = q_lat.shape[0]
    nq = q_lat.shape[1] // nreq
    lpad = c_new.shape[1]
    grid_spec = pltpu.PrefetchScalarGridSpec(
        num_scalar_prefetch=1, grid=(nreq,),
        in_specs=[
            pl.BlockSpec((n_half, nq, LANES), lambda b, pt: (0, b, 0)),
            pl.BlockSpec((nq, LANES), lambda b, pt: (b, 0)),
            pl.BlockSpec((1, lpad, kv_lora), lambda b, pt: (b, 0, 0)),
            pl.BlockSpec((1, lpad, QK_ROPE), lambda b, pt: (b, 0, 0)),
            pl.BlockSpec(memory_space=pl.ANY),
            pl.BlockSpec(memory_space=pl.ANY),
        ],
        out_specs=pl.BlockSpec((n_half, nq, LANES), lambda b, pt: (0, b, 0)),
        scratch_shapes=[
            pltpu.VMEM((2, n_pages, page, kv_lora), f32),
            pltpu.VMEM((2, n_pages, QK_ROPE, page), f32),
            pltpu.VMEM((n_pages * page, kv_lora), bf16),
            pltpu.VMEM((QK_ROPE, n_pages * page), bf16),
            pltpu.SemaphoreType.DMA((2,)),
            pltpu.SemaphoreType.DMA((2,)),
        ],
    )
    return pl.pallas_call(
        functools.partial(_sample_attn_kernel, n_pages=n_pages, page=page,
                          n_chunks=min(SAMPLE_ATTN_CHUNKS, n_pages)),
        grid_spec=grid_spec,
        out_shape=jax.ShapeDtypeStruct((n_half, nreq * nq, LANES), f32),
        compiler_params=_params("arbitrary"),
        name="sample_attn",
    )(page_table.reshape(-1), q_lat, q_rope, c_new, kr_new, cache_ckv, cache_krope_t)


def _mla_out_kernel(*refs, absorbed, rows):
    if absorbed:
        ol_ref, w_uv_ref, z_ref, x_ref, w_out_ref, post_g_ref, xo_ref = refs
        def head_rows(hd):
            halves = [ol_ref[c, pl.ds(hd, rows, stride=N_HEADS), :] for c in range(ol_ref.shape[0])]
            return jnp.concatenate(halves, axis=-1).astype(bf16)

        parts = []
        for g in range(N_HEADS // 2):
            parts.append(_dot(head_rows(2 * g), w_uv_ref[2 * g]) + _dot(head_rows(2 * g + 1), w_uv_ref[2 * g + 1]))
        o = jnp.concatenate(parts, axis=-1)
    else:
        o_ref, z_ref, x_ref, w_out_ref, post_g_ref, xo_ref = refs
        o = o_ref[...]
    g = (o.astype(f32) * jax.nn.silu(z_ref[...].astype(f32))).astype(bf16)
    m = _dot(g, w_out_ref[...])
    xo_ref[...] = x_ref[...] + _rms(m, post_g_ref[...])


def _mla_out(o, z, x, w_out, post_g, w_uv_pad=None, *, layer, tile):
    rows, d = x.shape
    z_dim = z.shape[1]
    nt = rows // tile
    absorbed = w_uv_pad is not None
    row_spec = lambda w: pl.BlockSpec((tile, w), lambda i: (i, 0))
    if absorbed:
        assert nt == 1
        in_specs = [_const_spec(o.shape), _const_spec(w_uv_pad.shape)]
        args = [o, w_uv_pad]
    else:
        in_specs = [row_spec(z_dim)]
        args = [o]
    in_specs += [row_spec(z_dim), row_spec(d), _layer_spec(w_out, layer), _const_spec((1, d))]
    args += [z, x, w_out, post_g]
    return pl.pallas_call(
        functools.partial(_mla_out_kernel, absorbed=absorbed, rows=rows),
        grid=(nt,), in_specs=in_specs, out_specs=row_spec(d),
        out_shape=jax.ShapeDtypeStruct((rows, d), f32),
        compiler_params=_params("arbitrary"),
        name="mla_out_absorbed" if absorbed else "mla_out",
    )(*args)


def _rope_tables(pos, scale):
    inv = ROPE_THETA ** (-jnp.arange(HALF_ROPE, dtype=f32) / HALF_ROPE)
    ang = pos.astype(f32)[:, None] * inv[None, :]
    cos, sin = jnp.cos(ang), jnp.sin(ang)
    r = pos.shape[0]
    zeros = lambda w: jnp.zeros((r, w), f32)
    c = jnp.concatenate([zeros(ROPE_LANE0), cos, cos, jnp.ones((r, QK_NOPE), f32)], axis=-1)
    s = jnp.concatenate([zeros(ROPE_LANE0), -sin, sin, zeros(QK_NOPE)], axis=-1)
    return c * scale, s * scale


def _head_blocks(w_rope, w_nope):
    blk = jnp.concatenate([w_rope, w_rope, w_nope], axis=-1)
    return blk.reshape(blk.shape[:-2] + (N_HEADS * LANES,))


def kernel(x_prompt, x_sample, cache_ckv, cache_krope, state_conv, page_table, meta_tokens,
           pre_norm_g, post_norm_g, w_in_conv, conv_w, w_out_conv, kv_norm_g, w_dkv,
           kv_lat_norm_g, w_uk, w_uv, w_in_mla, q_norm_g, w_uq, w_out_mla):
    bp, seq, d = x_prompt.shape
    bs, ls, _ = x_sample.shape
    n_a = w_in_conv.shape[0]
    n_b = w_in_mla.shape[0]
    c_dim = conv_w.shape[2]
    kv_lora = kv_lat_norm_g.shape[0]
    q_lora = q_norm_g.shape[1]
    past_len = page_table.shape[1] * cache_ckv.shape[1]

    row = lambda v: v.reshape(1, -1).astype(f32)
    w_in_conv_b = w_in_conv.astype(bf16)
    w_out_conv_b = w_out_conv.astype(bf16)
    w_in_mla_b = w_in_mla.astype(bf16)
    w_out_mla_b = w_out_mla.astype(bf16)
    w_dkv_pad = jnp.concatenate([w_dkv, w_dkv[:, kv_lora:], jnp.zeros((d, QK_NOPE), w_dkv.dtype)],
                                axis=-1).astype(bf16)
    uq = w_uq.reshape(n_b, q_lora, N_HEADS, QK_NOPE + QK_ROPE)
    w_uq_pad = _head_blocks(uq[..., QK_NOPE:], uq[..., :QK_NOPE]).astype(bf16)
    w_uk_blk = _head_blocks(jnp.zeros((kv_lora, N_HEADS, QK_ROPE), w_uk.dtype), w_uk)
    w_uk_pad = w_uk_blk.astype(bf16)
    w_ukt_pad = jnp.transpose(w_uk_blk.reshape(kv_lora, N_HEADS, LANES), (1, 2, 0)).astype(bf16)
    uv = jnp.transpose(w_uv, (1, 0, 2))
    zv = jnp.zeros_like(uv)
    even = (jnp.arange(N_HEADS) % 2 == 0)[:, None, None]
    w_uv_pad = jnp.concatenate([jnp.where(even, uv, zv), jnp.where(even, zv, uv)], axis=-1).astype(bf16)
    w_uv_blk = jnp.transpose(w_uv_pad, (1, 0, 2)).reshape(kv_lora, N_HEADS * LANES)

    def trunk_a(x, inits, *, nseq, tile, shift):
        states = []
        for l in range(n_a):
            x, st = _conv_layer(x, inits[l], row(pre_norm_g[l]), row(post_norm_g[l]), w_in_conv_b,
                                conv_w[l], w_out_conv_b, layer=l, nseq=nseq, tile=tile, shift=shift)
            states.append(st)
        return x, states

    zero_init = jnp.zeros((1, SUBLANES, c_dim), f32)
    xm, meta_states = trunk_a(meta_tokens.astype(f32), [zero_init] * n_a, nseq=1, tile=N_META, shift=1)

    tile_p = min(ROW_TILE, seq)
    inits_p = [jnp.concatenate([jnp.zeros((1, SUBLANES - 2, c_dim), f32), st], axis=1) for st in meta_states]
    xp, prompt_states = trunk_a(x_prompt.reshape(bp * seq, d), inits_p, nseq=bp, tile=tile_p, shift=1)
    t_pos = N_META + seq
    t_pad = -(-t_pos // tile_p) * tile_p
    x_pos = jnp.concatenate([jnp.broadcast_to(xm[None], (bp, N_META, d)), xp.reshape(bp, seq, d),
                             jnp.zeros((bp, t_pad - t_pos, d), f32)], axis=1)
    tabs_pos = _rope_tables(jnp.arange(t_pad, dtype=jnp.int32), 1.0)
    ckv_prompt, krope_prompt, k_p, v_p = _latent(x_pos, row(kv_norm_g), w_dkv_pad, row(kv_lat_norm_g), tabs_pos,
                                                 w_uk_pad, w_uv_blk, t_out=t_pos, tile=tile_p)
    tabs_q = tuple(t[N_META:t_pos] * (SOFTMAX_SCALE * LOG2_E) for t in tabs_pos)
    attn_tile = min(ATTN_TILE, seq)
    for j in range(n_b):
        l = n_a + j
        q, z = _mla_query(xp, row(pre_norm_g[l]), w_in_mla_b, row(q_norm_g[j]), w_uq_pad, tabs_q,
                          layer=j, nseq=bp, tile=tile_p)
        o = _prompt_attn(q.reshape(bp, seq, -1), k_p, v_p, tile=attn_tile)
        xp = _mla_out(o.reshape(bp * seq, -1), z, xp, w_out_mla_b, row(post_norm_g[l]), layer=j, tile=tile_p)
    y_prompt = xp.reshape(bp, seq, d)
    conv_prompt = jnp.stack([st for st in prompt_states])

    rs = bs * ls
    xs = jnp.transpose(x_sample, (1, 0, 2)).reshape(rs, d)
    inits_s = [jnp.transpose(state_conv[l], (1, 0, 2)).reshape(1, (CONV_WIDTH - 1) * bs, c_dim) for l in range(n_a)]
    xs, sample_states = trunk_a(xs, inits_s, nseq=1, tile=rs, shift=bs)
    conv_sample = jnp.stack([jnp.transpose(st.reshape(CONV_WIDTH - 1, bs, c_dim), (1, 0, 2)) for st in sample_states])
    xs = jnp.transpose(xs.reshape(ls, bs, d), (1, 0, 2)).reshape(rs, d)
    pos_s = jnp.tile(past_len + jnp.arange(ls, dtype=jnp.int32), bs)
    c_s, kr_s = _latent(xs[None], row(kv_norm_g), w_dkv_pad, row(kv_lat_norm_g), _rope_tables(pos_s, 1.0),
                        t_out=rs, tile=rs)
    lpad = 16
    c_new = jnp.pad(c_s.reshape(bs, ls, kv_lora), ((0, 0), (0, lpad - ls), (0, 0)))
    kr_new = jnp.pad(kr_s.reshape(bs, ls, QK_ROPE), ((0, 0), (0, lpad - ls), (0, 0)))
    tabs_qs = _rope_tables(pos_s, SOFTMAX_SCALE * LOG2_E)
    cache_krope_t = jnp.swapaxes(cache_krope, 1, 2)
    for j in range(n_b):
        l = n_a + j
        q_lat, q_rope, z = _mla_query(xs, row(pre_norm_g[l]), w_in_mla_b, row(q_norm_g[j]), w_uq_pad,
                                      tabs_qs, w_ukt_pad, layer=j, nseq=1, tile=rs)
        o_lat = _sample_attn(page_table, q_lat, q_rope, c_new, kr_new, cache_ckv, cache_krope_t)
        xs = _mla_out(o_lat, z, xs, w_out_mla_b, row(post_norm_g[l]), w_uv_pad, layer=j, tile=rs)
    y_sample = xs.reshape(bs, ls, d)
    ckv_sample = c_s.reshape(bs, ls, kv_lora)
    krope_sample = kr_s.reshape(bs, ls, QK_ROPE)

    return (y_prompt, y_sample, ckv_prompt, krope_prompt, conv_prompt, ckv_sample, krope_sample, conv_sample)
```
